```python
import jax, jax.numpy as jnp
from jax import lax
import numpy as np

D_MODEL = 2048
BATCH = 2
SEQ = 8192
DEPTH = 4

N_MIXERS = 3
BLOCK = 128
EPS = 1e-6
NEG = -1e30
ROPE_THETA = 10000.0

A_HEAD_DIM = 64
A_HEADS = D_MODEL // A_HEAD_DIM
A_KV_HEADS = A_HEADS // 8
WINDOW = 128
A_IN = (A_HEADS + 2 * A_KV_HEADS) * A_HEAD_DIM
A_OUT = A_HEADS * A_HEAD_DIM

B_HEAD_DIM = 64
B_HEADS = D_MODEL // B_HEAD_DIM
B_IN = 3 * B_HEADS * B_HEAD_DIM + B_HEADS
B_OUT = B_HEADS * B_HEAD_DIM

C_HEADS = 4
C_V_DIM = D_MODEL // C_HEADS
C_QK_DIM = C_V_DIM // 2
C_CHUNK = 64
C_IN = 2 * C_HEADS * C_QK_DIM + 2 * C_HEADS * C_V_DIM + 2 * C_HEADS
C_OUT = C_HEADS * C_V_DIM

D_FF = ((8 * D_MODEL // 3 + 255) // 256) * 256
CONV_WIDTH = 3

N_A = (DEPTH + 2) // 3
N_B = (DEPTH + 1) // 3
N_C = DEPTH // 3

kernel_name = "hybrid_swa_fox_mlstm_convffn_adaln"


def rms_norm(x, g):
    xf = x.astype(jnp.float32)
    y = xf * lax.rsqrt(jnp.mean(xf * xf, axis=-1, keepdims=True) + EPS)
    return y.astype(x.dtype) * g


def modulate(h, shift, scale):
    return h * (1 + scale[:, None, :]) + shift[:, None, :]


def rope(t, positions):
    dh = t.shape[-1]
    inv_freq = ROPE_THETA ** (-jnp.arange(0, dh, 2, dtype=jnp.float32) / dh)
    ang = positions.astype(jnp.float32)[..., None] * inv_freq
    cos, sin = jnp.cos(ang)[:, :, None, :], jnp.sin(ang)[:, :, None, :]
    t1, t2 = jnp.split(t.astype(jnp.float32), 2, axis=-1)
    out = jnp.concatenate([t1 * cos - t2 * sin, t2 * cos + t1 * sin], axis=-1)
    return out.astype(t.dtype)


def swa_sink_attention(h, w_in, sinks, w_out, positions):
    B, S, _ = h.shape
    G = A_HEADS // A_KV_HEADS
    nb = S // BLOCK
    nq, nkv = A_HEADS * A_HEAD_DIM, A_KV_HEADS * A_HEAD_DIM
    q, k, v = jnp.split(h @ w_in, [nq, nq + nkv], axis=-1)
    q = rope(q.reshape(B, S, A_HEADS, A_HEAD_DIM), positions)
    k = rope(k.reshape(B, S, A_KV_HEADS, A_HEAD_DIM), positions)
    v = v.reshape(B, S, A_KV_HEADS, A_HEAD_DIM)
    q = q.reshape(B, nb, BLOCK, A_KV_HEADS, G, A_HEAD_DIM)

    def band(t):
        t = t.reshape(B, nb, BLOCK, A_KV_HEADS, A_HEAD_DIM)
        prev = jnp.pad(t, ((0, 0), (1, 0), (0, 0), (0, 0), (0, 0)))[:, :-1]
        return jnp.concatenate([prev, t], axis=2)

    kb, vb = band(k), band(v)
    scores = jnp.einsum('bnqhgd,bnkhd->bnhgqk', q, kb).astype(jnp.float32) * (A_HEAD_DIM ** -0.5)
    qi = jnp.arange(BLOCK)[:, None]
    kj = jnp.arange(2 * BLOCK)[None, :]
    rel = qi + BLOCK - kj
    in_win = (rel >= 0) & (rel < WINDOW)
    kpos = (jnp.arange(nb)[:, None, None] - 1) * BLOCK + kj[None]
    mask = in_win[None] & (kpos >= 0)
    scores = jnp.where(mask[None, :, None, None], scores, NEG)
    sink = jnp.broadcast_to(sinks.reshape(A_KV_HEADS, G).astype(jnp.float32)[None, None, :, :, None, None],
                            scores.shape[:-1] + (1,))
    probs = jax.nn.softmax(jnp.concatenate([scores, sink], axis=-1), axis=-1)[..., :-1]
    out = jnp.einsum('bnhgqk,bnkhd->bnqhgd', probs.astype(vb.dtype), vb)
    return out.reshape(B, S, A_OUT) @ w_out


def forgetting_attention(h, w_in, f_bias, w_out):
    B, S, _ = h.shape
    nb = S // BLOCK
    HD = B_HEADS * B_HEAD_DIM
    q, k, v, f_logit = jnp.split(h @ w_in, [HD, 2 * HD, 3 * HD], axis=-1)
    q = q.reshape(B, S, B_HEADS, B_HEAD_DIM)
    k = k.reshape(B, S, B_HEADS, B_HEAD_DIM)
    v = v.reshape(B, S, B_HEADS, B_HEAD_DIM)
    log_f = jax.nn.log_sigmoid((f_logit + f_bias).astype(jnp.float32))
    F = jnp.cumsum(log_f, axis=1)
    Fk = F.transpose(0, 2, 1)
    qb = q.reshape(B, nb, BLOCK, B_HEADS, B_HEAD_DIM).transpose(1, 0, 2, 3, 4)
    Fq = F.reshape(B, nb, BLOCK, B_HEADS).transpose(1, 0, 3, 2)
    key_pos = jnp.arange(S)

    def block(args):
        qi, Fi, bi = args
        s = jnp.einsum('bqhd,bkhd->bhqk', qi, k).astype(jnp.float32) * (B_HEAD_DIM ** -0.5)
        s = s + (Fi[..., :, None] - Fk[:, :, None, :])
        q_pos = bi * BLOCK + jnp.arange(BLOCK)
        s = jnp.where((key_pos[None, :] <= q_pos[:, None])[None, None], s, NEG)
        p = jax.nn.softmax(s, axis=-1).astype(v.dtype)
        return jnp.einsum('bhqk,bkhd->bqhd', p, v)

    out = lax.map(block, (qb, Fq, jnp.arange(nb)))
    out = out.transpose(1, 0, 2, 3, 4).reshape(B, S, B_OUT)
    return out @ w_out


def mlstm(h, w_in, gate_bias, w_out):
    B, S, _ = h.shape
    L = C_CHUNK
    nc = S // L
    Hq, Hv = C_HEADS * C_QK_DIM, C_HEADS * C_V_DIM
    q, k, v, o, gates = jnp.split(h @ w_in, [Hq, 2 * Hq, 2 * Hq + Hv, 2 * Hq + 2 * Hv], axis=-1)
    f32 = jnp.float32
    q = q.astype(f32).reshape(B, S, C_HEADS, C_QK_DIM) * (C_QK_DIM ** -0.5)
    k = k.astype(f32).reshape(B, S, C_HEADS, C_QK_DIM)
    v = v.astype(f32).reshape(B, S, C_HEADS, C_V_DIM)
    gates = (gates + gate_bias).astype(f32)
    i_pre = gates[..., :C_HEADS]
    log_f = jax.nn.log_sigmoid(gates[..., C_HEADS:])

    def chunks(t):
        return t.reshape(B, nc, L, C_HEADS, t.shape[-1]).transpose(1, 0, 3, 2, 4)

    def chunks_g(t):
        return t.reshape(B, nc, L, C_HEADS).transpose(1, 0, 3, 2)

    tri = jnp.tril(jnp.ones((L, L), dtype=bool))

    def step(carry, xs):
        C, n, m = carry
        qx, kx, vx, ix, fx = xs
        b = jnp.cumsum(fx, axis=-1)
        g = b[..., -1]
        Dm = jnp.where(tri, b[..., :, None] - b[..., None, :] + ix[..., None, :], NEG)
        inter = b + m[..., None]
        m_t = jnp.maximum(inter, jnp.max(Dm, axis=-1))
        A = jnp.exp(Dm - m_t[..., None]) * jnp.einsum('bhtd,bhsd->bhts', qx, kx)
        sc = jnp.exp(inter - m_t)
        num = sc[..., None] * jnp.einsum('bhtd,bhde->bhte', qx, C) + jnp.einsum('bhts,bhse->bhte', A, vx)
        den = sc * jnp.einsum('bhtd,bhd->bht', qx, n) + jnp.sum(A, axis=-1)
        h_t = num / jnp.maximum(jnp.abs(den), jnp.exp(-m_t))[..., None]
        wlog = g[..., None] - b + ix
        m_new = jnp.maximum(g + m, jnp.max(wlog, axis=-1))
        decay = jnp.exp(g + m - m_new)
        w_s = jnp.exp(wlog - m_new[..., None])
        C_new = decay[..., None, None] * C + jnp.einsum('bhs,bhsd,bhse->bhde', w_s, kx, vx)
        n_new = decay[..., None] * n + jnp.einsum('bhs,bhsd->bhd', w_s, kx)
        return (C_new, n_new, m_new), h_t

    init = (jnp.zeros((B, C_HEADS, C_QK_DIM, C_V_DIM), f32),
            jnp.zeros((B, C_HEADS, C_QK_DIM), f32),
            jnp.zeros((B, C_HEADS), f32))
    _, hs = lax.scan(step, init, (chunks(q), chunks(k), chunks(v), chunks_g(i_pre), chunks_g(log_f)))
    hs = hs.transpose(1, 0, 3, 2, 4).reshape(B, S, C_OUT).astype(h.dtype)
    return (jax.nn.sigmoid(o) * hs) @ w_out


def conv_glu_ffn(h, w_up, conv_w, conv_b, w_down):
    S = h.shape[1]
    u = h @ w_up
    up = jnp.pad(u, ((0, 0), (CONV_WIDTH - 1, 0), (0, 0)))
    y = up[:, 0:S] * conv_w[0]
    for j in range(1, CONV_WIDTH):
        y = y + up[:, j:j + S] * conv_w[j]
    y = y + conv_b
    gate, val = jnp.split(y, 2, axis=-1)
    return (jax.nn.silu(gate) * val) @ w_down


def setup_inputs(seed: int = 0) -> dict:
    key = jax.random.key(seed)
    ks = jax.random.split(key, 24)
    f32 = jnp.float32
    nrm = lambda k, shape, s: jax.random.normal(k, shape, f32) * s
    x = nrm(ks[0], (BATCH, SEQ, D_MODEL), 1.0)
    c = nrm(ks[1], (BATCH, D_MODEL), 1.0)
    offset = jax.random.randint(ks[2], (BATCH, 1), 0, 4096, dtype=jnp.int32)
    positions = jnp.arange(SEQ, dtype=jnp.int32)[None, :] + offset
    mod_w = nrm(ks[3], (DEPTH, D_MODEL, 6 * D_MODEL), 0.5 * D_MODEL ** -0.5)
    mod_b = nrm(ks[4], (DEPTH, 6 * D_MODEL), 0.02)
    norm_g = 1.0 + nrm(ks[5], (DEPTH, 2, D_MODEL), 0.02)
    final_g = 1.0 + nrm(ks[6], (D_MODEL,), 0.02)
    a_w_in = nrm(ks[7], (N_A, D_MODEL, A_IN), D_MODEL ** -0.5)
    a_sinks = nrm(ks[8], (N_A, A_HEADS), 1.0)
    a_w_out = nrm(ks[9], (N_A, A_OUT, D_MODEL), A_OUT ** -0.5)
    b_w_in = nrm(ks[10], (N_B, D_MODEL, B_IN), D_MODEL ** -0.5)
    b_f_bias = jax.random.uniform(ks[11], (N_B, B_HEADS), f32, 1.0, 4.0)
    b_w_out = nrm(ks[12], (N_B, B_OUT, D_MODEL), B_OUT ** -0.5)
    c_w_in = nrm(ks[13], (N_C, D_MODEL, C_IN), D_MODEL ** -0.5)
    c_gate_bias = jnp.concatenate([nrm(ks[14], (N_C, C_HEADS), 0.1),
                                   jax.random.uniform(ks[15], (N_C, C_HEADS), f32, 3.0, 6.0)], axis=-1)
    c_w_out = nrm(ks[16], (N_C, C_OUT, D_MODEL), C_OUT ** -0.5)
    ffn_w_up = nrm(ks[17], (DEPTH, D_MODEL, 2 * D_FF), D_MODEL ** -0.5)
    center = jnp.zeros((CONV_WIDTH, 1), f32).at[CONV_WIDTH - 1].set(1.0)
    ffn_conv_w = center[None] + nrm(ks[18], (DEPTH, CONV_WIDTH, 2 * D_FF), 0.2)
    ffn_conv_b = nrm(ks[19], (DEPTH, 2 * D_FF), 0.02)
    ffn_w_down = nrm(ks[20], (DEPTH, D_FF, D_MODEL), D_FF ** -0.5)
    return {"x": x, "c": c, "positions": positions, "mod_w": mod_w, "mod_b": mod_b,
            "norm_g": norm_g, "final_g": final_g,
            "a_w_in": a_w_in, "a_sinks": a_sinks, "a_w_out": a_w_out,
            "b_w_in": b_w_in, "b_f_bias": b_f_bias, "b_w_out": b_w_out,
            "c_w_in": c_w_in, "c_gate_bias": c_gate_bias, "c_w_out": c_w_out,
            "ffn_w_up": ffn_w_up, "ffn_conv_w": ffn_conv_w, "ffn_conv_b": ffn_conv_b,
            "ffn_w_down": ffn_w_down}


def reference(x, c, positions, mod_w, mod_b, norm_g, final_g,
              a_w_in, a_sinks, a_w_out, b_w_in, b_f_bias, b_w_out,
              c_w_in, c_gate_bias, c_w_out, ffn_w_up, ffn_conv_w, ffn_conv_b, ffn_w_down):
    c_act = jax.nn.silu(c)
    for i in range(DEPTH):
        mod = c_act @ mod_w[i] + mod_b[i]
        sh1, sc1, g1, sh2, sc2, g2 = jnp.split(mod, 6, axis=-1)
        hm = modulate(rms_norm(x, norm_g[i, 0]), sh1, sc1)
        kind, j = i % N_MIXERS, i // N_MIXERS
        if kind == 0:
            y = swa_sink_attention(hm, a_w_in[j], a_sinks[j], a_w_out[j], positions)
        elif kind == 1:
            y = forgetting_attention(hm, b_w_in[j], b_f_bias[j], b_w_out[j])
        else:
            y = mlstm(hm, c_w_in[j], c_gate_bias[j], c_w_out[j])
        x = x + g1[:, None, :] * y
        hf = modulate(rms_norm(x, norm_g[i, 1]), sh2, sc2)
        x = x + g2[:, None, :] * conv_glu_ffn(hf, ffn_w_up[i], ffn_conv_w[i], ffn_conv_b[i], ffn_w_down[i])
    return rms_norm(x, final_g)
```

```python
import functools

import jax
import jax.numpy as jnp
from jax import lax
from jax.experimental import pallas as pl
from jax.experimental.pallas import tpu as pltpu

F32 = jnp.float32
BF16 = jnp.bfloat16

EPS = 1e-6
NEG = -1e30
ROPE_THETA = 10000.0
HEAD_DIM = 64
HALF = HEAD_DIM // 2
GQA_GROUP = 8
WINDOW = 128
C_HEADS = 4
N_MIXERS = 3
CONV_WIDTH = 3
LANES = 128
SUBLANES = 8
VMEM_LIMIT = 56 * 1024 * 1024


def _pick(dim, prefs):
    for p in prefs:
        if dim % p == 0:
            return p
    return dim


def _params(*sem):
    return pltpu.CompilerParams(dimension_semantics=sem, vmem_limit_bytes=VMEM_LIMIT)


def _lane_iota(shape):
    return lax.broadcasted_iota(jnp.int32, shape, len(shape) - 1)


def _log_sigmoid(x):
    return jnp.minimum(x, 0.0) - jnp.log1p(jnp.exp(-jnp.abs(x)))


def _tril_f32(n):
    r = lax.broadcasted_iota(jnp.int32, (n, n), 0)
    c = lax.broadcasted_iota(jnp.int32, (n, n), 1)
    return (c <= r).astype(F32)


def _mod_kernel(c_ref, w_ref, b_ref, o_ref):
    c = c_ref[...]
    act = (c * jax.nn.sigmoid(c)).astype(BF16)
    o_ref[...] = jnp.dot(act, w_ref[...].astype(BF16), preferred_element_type=F32) + b_ref[...]


def _modulation(c, mod_w, mod_b):
    depth, d, n = mod_w.shape
    b = c.shape[0]
    cp = jnp.zeros((SUBLANES, d), F32).at[:b].set(c)
    tn = _pick(n, (1024, 512, 256, 128))
    return pl.pallas_call(
        _mod_kernel,
        grid=(depth, n // tn),
        in_specs=[
            pl.BlockSpec((SUBLANES, d), lambda l, j: (0, 0)),
            pl.BlockSpec((None, d, tn), lambda l, j: (l, 0, j)),
            pl.BlockSpec((None, 1, tn), lambda l, j: (l, 0, j)),
        ],
        out_specs=pl.BlockSpec((None, SUBLANES, tn), lambda l, j: (l, 0, j)),
        out_shape=jax.ShapeDtypeStruct((depth, SUBLANES, n), F32),
        compiler_params=_params("parallel", "parallel"),
        name="modulation",
    )(cp, mod_w, mod_b.reshape(depth, 1, n))


def _rope_table_kernel(pos_ref, freq_ref, cos_ref, sin_ref):
    ang = pos_ref[...] * freq_ref[...]
    lane = _lane_iota(ang.shape)
    cos_ref[...] = jnp.cos(ang)
    s = jnp.sin(ang)
    sin_ref[...] = jnp.where((lane & (HEAD_DIM - 1)) < HALF, -s, s)


def _rope_tables(positions):
    t = positions.size
    pos = positions.reshape(t, 1).astype(F32)
    inv_freq = ROPE_THETA ** (-jnp.arange(0, HEAD_DIM, 2, dtype=F32) / HEAD_DIM)
    freq = jnp.tile(inv_freq, LANES // HALF).reshape(1, LANES)
    ts = _pick(t, (1024, 512, 256, 128))
    return pl.pallas_call(
        _rope_table_kernel,
        grid=(t // ts,),
        in_specs=[pl.BlockSpec((ts, 1), lambda i: (i, 0)), pl.BlockSpec((1, LANES), lambda i: (0, 0))],
        out_specs=[pl.BlockSpec((ts, LANES), lambda i: (i, 0))] * 2,
        out_shape=[jax.ShapeDtypeStruct((t, LANES), F32)] * 2,
        compiler_params=_params("parallel"),
        name="rope_tables",
    )(pos, freq)


def _norm_mod(x_ref, g_ref, sh_ref, sc_ref):
    x = x_ref[...]
    y = x * lax.rsqrt(jnp.mean(x * x, axis=-1, keepdims=True) + EPS)
    y = y * g_ref[...]
    return (y * (1.0 + sc_ref[...]) + sh_ref[...]).astype(BF16)


def _rope(a, cos, sins):
    lane = _lane_iota(a.shape)
    swapped = jnp.where((lane & (HEAD_DIM - 1)) < HALF, pltpu.roll(a, LANES - HALF, 1), pltpu.roll(a, HALF, 1))
    return a * cos + swapped * sins


def _proj_a_kernel(x_ref, g_ref, sh_ref, sc_ref, w_ref, cos_ref, sin_ref, o_ref, *, nq, nkv):
    h = _norm_mod(x_ref, g_ref, sh_ref, sc_ref)
    acc = jnp.dot(h, w_ref[...], preferred_element_type=F32)
    cos = cos_ref[...]
    sins = sin_ref[...]
    for c in range((nq + nkv) // LANES):
        a = _rope(acc[:, c * LANES:(c + 1) * LANES], cos, sins)
        if c * LANES < nq:
            a = a * (HEAD_DIM ** -0.5)
        o_ref[:, c * LANES:(c + 1) * LANES] = a.astype(BF16)
    o_ref[:, nq + nkv:] = acc[:, nq + nkv:].astype(BF16)


def _proj_a(x, g, sh, sc, w, cos, sins, seq):
    t, d = x.shape
    n = w.shape[1]
    nq = d
    nkv = (n - nq) // 2
    tm = _pick(seq, (512, 256, 128))
    per_b = seq // tm
    return pl.pallas_call(
        functools.partial(_proj_a_kernel, nq=nq, nkv=nkv),
        grid=(t // tm,),
        in_specs=[
            pl.BlockSpec((tm, d), lambda i: (i, 0)),
            pl.BlockSpec((1, d), lambda i: (0, 0)),
            pl.BlockSpec((None, 1, d), lambda i: (i // per_b, 0, 0)),
            pl.BlockSpec((None, 1, d), lambda i: (i // per_b, 0, 0)),
            pl.BlockSpec((d, n), lambda i: (0, 0)),
            pl.BlockSpec((tm, LANES), lambda i: (i, 0)),
            pl.BlockSpec((tm, LANES), lambda i: (i, 0)),
        ],
        out_specs=pl.BlockSpec((tm, n), lambda i: (i, 0)),
        out_shape=jax.ShapeDtypeStruct((t, n), BF16),
        compiler_params=_params("parallel"),
        name="proj_swa",
    )(x, g, sh, sc, w, cos, sins)


def _proj_gate_kernel(x_ref, g_ref, sh_ref, sc_ref, w_ref, wg_ref, bg_ref, o_ref, og_ref, h_scr, *,
                      q_tiles, q_scale):
    j = pl.program_id(1)

    @pl.when(j == 0)
    def _():
        h_scr[...] = _norm_mod(x_ref, g_ref, sh_ref, sc_ref)
        og_ref[...] = jnp.dot(h_scr[...], wg_ref[...], preferred_element_type=F32) + bg_ref[...]

    acc = jnp.dot(h_scr[...], w_ref[...], preferred_element_type=F32)
    o_ref[...] = (acc * jnp.where(j < q_tiles, q_scale, 1.0)).astype(BF16)


def _proj_gate(x, g, sh, sc, w, wg, bg, seq, q_cols, q_scale):
    t, d = x.shape
    n = w.shape[1]
    tm = _pick(seq, (1024, 512, 256, 128))
    tn = _pick(q_cols, (1024, 512, 256, 128))
    per_b = seq // tm
    return pl.pallas_call(
        functools.partial(_proj_gate_kernel, q_tiles=q_cols // tn, q_scale=q_scale),
        grid=(t // tm, n // tn),
        in_specs=[
            pl.BlockSpec((tm, d), lambda i, j: (i, 0)),
            pl.BlockSpec((1, d), lambda i, j: (0, 0)),
            pl.BlockSpec((None, 1, d), lambda i, j: (i // per_b, 0, 0)),
            pl.BlockSpec((None, 1, d), lambda i, j: (i // per_b, 0, 0)),
            pl.BlockSpec((d, tn), lambda i, j: (0, j)),
            pl.BlockSpec((d, LANES), lambda i, j: (0, 0)),
            pl.BlockSpec((1, LANES), lambda i, j: (0, 0)),
        ],
        out_specs=[pl.BlockSpec((tm, tn), lambda i, j: (i, j)), pl.BlockSpec((tm, LANES), lambda i, j: (i, 0))],
        out_shape=[jax.ShapeDtypeStruct((t, n), BF16), jax.ShapeDtypeStruct((t, LANES), F32)],
        scratch_shapes=[pltpu.VMEM((tm, d), BF16)],
        compiler_params=_params("parallel", "arbitrary"),
        name="proj_gate",
    )(x, g, sh, sc, w, wg, bg)


def _proj_res_kernel(y_ref, w_ref, x_ref, g_ref, o_ref):
    acc = jnp.dot(y_ref[...], w_ref[...], preferred_element_type=F32)
    o_ref[...] = x_ref[...] + g_ref[...] * acc


def _proj_res(y, w, x, gate, seq):
    t, k = y.shape
    n = w.shape[1]
    tm = _pick(seq, (1024, 512, 256, 128))
    tn = _pick(n, (512, 256, 128))
    per_b = seq // tm
    return pl.pallas_call(
        _proj_res_kernel,
        grid=(t // tm, n // tn),
        in_specs=[
            pl.BlockSpec((tm, k), lambda i, j: (i, 0)),
            pl.BlockSpec((k, tn), lambda i, j: (0, j)),
            pl.BlockSpec((tm, tn), lambda i, j: (i, j)),
            pl.BlockSpec((None, 1, tn), lambda i, j: (i // per_b, 0, j)),
        ],
        out_specs=pl.BlockSpec((tm, tn), lambda i, j: (i, j)),
        out_shape=jax.ShapeDtypeStruct((t, n), F32),
        input_output_aliases={2: 0},
        compiler_params=_params("parallel", "arbitrary"),
        name="proj_residual",
    )(y, w, x, gate)


def _ffn_up_kernel(x_ref, g_ref, sh_ref, sc_ref, wg_ref, wv_ref, cg_ref, cv_ref, o_ref, h_scr, tg_scr, tv_scr, *,
                   per_b):
    i = pl.program_id(0)
    j = pl.program_id(1)
    tm = o_ref.shape[0]

    @pl.when(j == 0)
    def _():
        h_scr[...] = _norm_mod(x_ref, g_ref, sh_ref, sc_ref)

    @pl.when(i == 0)
    def _():
        tg_scr[j] = jnp.zeros(tg_scr.shape[1:], F32)
        tv_scr[j] = jnp.zeros(tv_scr.shape[1:], F32)

    h = h_scr[...]
    ug = jnp.dot(h, wg_ref[...], preferred_element_type=F32)
    uv = jnp.dot(h, wv_ref[...], preferred_element_type=F32)

    def conv(u, c_ref, shifted1, shifted2):
        return u * c_ref[2:3, :] + shifted1 * c_ref[1:2, :] + shifted2 * c_ref[0:1, :] + c_ref[3:4, :]

    def glu(yg, yv):
        return (yg * jax.nn.sigmoid(yg) * yv).astype(o_ref.dtype)

    yg = conv(ug, cg_ref, pltpu.roll(ug, 1, 0), pltpu.roll(ug, 2, 0))
    yv = conv(uv, cv_ref, pltpu.roll(uv, 1, 0), pltpu.roll(uv, 2, 0))
    o_ref[...] = glu(yg, yv)

    first = (i % per_b) == 0
    row = lax.broadcasted_iota(jnp.int32, (SUBLANES, ug.shape[1]), 0)

    def head_rows(u, t_scr, c_ref):
        top = u[0:SUBLANES, :]
        tail = jnp.where(first, 0.0, t_scr[j])
        s1 = jnp.where(row == 0, pltpu.roll(tail, 1, 0), pltpu.roll(top, 1, 0))
        s2 = jnp.where(row <= 1, pltpu.roll(tail, 2, 0), pltpu.roll(top, 2, 0))
        t_scr[j] = u[tm - SUBLANES:tm, :]
        return conv(top, c_ref, s1, s2)

    o_ref[0:SUBLANES, :] = glu(head_rows(ug, tg_scr, cg_ref), head_rows(uv, tv_scr, cv_ref))


def _ffn_up(x, g, sh, sc, w_up, conv_tab, seq):
    t, d = x.shape
    f = w_up.shape[1] // 2
    tm = _pick(seq, (1024, 512, 256, 128))
    tn = _pick(f, (512, 256, 128))
    nj = f // tn
    per_b = seq // tm
    return pl.pallas_call(
        functools.partial(_ffn_up_kernel, per_b=per_b),
        grid=(t // tm, nj),
        in_specs=[
            pl.BlockSpec((tm, d), lambda i, j: (i, 0)),
            pl.BlockSpec((1, d), lambda i, j: (0, 0)),
            pl.BlockSpec((None, 1, d), lambda i, j: (i // per_b, 0, 0)),
            pl.BlockSpec((None, 1, d), lambda i, j: (i // per_b, 0, 0)),
            pl.BlockSpec((d, tn), lambda i, j: (0, j)),
            pl.BlockSpec((d, tn), lambda i, j: (0, j + nj)),
            pl.BlockSpec((SUBLANES, tn), lambda i, j: (0, j)),
            pl.BlockSpec((SUBLANES, tn), lambda i, j: (0, j + nj)),
        ],
        out_specs=pl.BlockSpec((tm, tn), lambda i, j: (i, j)),
        out_shape=jax.ShapeDtypeStruct((t, f), BF16),
        scratch_shapes=[
            pltpu.VMEM((tm, d), BF16),
            pltpu.VMEM((nj, SUBLANES, tn), F32),
            pltpu.VMEM((nj, SUBLANES, tn), F32),
        ],
        compiler_params=_params("arbitrary", "arbitrary"),
        name="ffn_up_conv_glu",
    )(x, g, sh, sc, w_up, w_up, conv_tab, conv_tab)


def _swa_kernel(q_ref, kp_ref, kc_ref, vp_ref, vc_ref, sink_ref, o_ref, *, heads):
    n = pl.program_id(1)
    blk = q_ref.shape[0]
    kb = jnp.concatenate([kp_ref[...], kc_ref[...]], axis=0).astype(F32)
    vb = jnp.concatenate([vp_ref[...], vc_ref[...]], axis=0).astype(F32)
    lane = _lane_iota((2 * blk, LANES))
    low = lane < HEAD_DIM
    qi = lax.broadcasted_iota(jnp.int32, (blk, 2 * blk), 0)
    kj = lax.broadcasted_iota(jnp.int32, (blk, 2 * blk), 1)
    rel = qi + blk - kj
    first_valid_key = jnp.where(n > 0, 0, blk)
    mask = (rel >= 0) & (rel < WINDOW) & (kj >= first_valid_key)
    qlow = _lane_iota((blk, LANES)) < HEAD_DIM
    contract_lanes = (((1,), (1,)), ((), ()))

    kv_cache = {}

    def kv_group(g):
        if g not in kv_cache:
            c, half = divmod(g, 2)
            kc = kb[:, c * LANES:(c + 1) * LANES]
            vc = vb[:, c * LANES:(c + 1) * LANES]
            kr = pltpu.roll(kc, HEAD_DIM, 1)
            vr = pltpu.roll(vc, HEAD_DIM, 1)
            k2 = (jnp.where(low, kc, kr) if half == 0 else jnp.where(low, kr, kc)).astype(BF16)
            v_lo, v_hi = (vc, vr) if half == 0 else (vr, vc)
            v_even = jnp.where(low, v_lo, 1.0).astype(BF16)
            v_odd = jnp.where(low, 1.0, v_hi).astype(BF16)
            kv_cache[g] = (k2, v_even, v_odd)
        return kv_cache[g]

    for p in range(heads // 2):
        k2, v_even, v_odd = kv_group((2 * p) // GQA_GROUP)
        qp = q_ref[:, p * LANES:(p + 1) * LANES]
        outs = []
        for par in range(2):
            hq = 2 * p + par
            qh = jnp.where(qlow if par == 0 else ~qlow, qp, jnp.zeros_like(qp))
            s = lax.dot_general(qh, k2, contract_lanes, preferred_element_type=F32)
            s = jnp.where(mask, s, NEG)
            sink = sink_ref[0:1, hq:hq + 1]
            m = jnp.maximum(jnp.max(s, axis=-1, keepdims=True), sink)
            pr = jnp.exp((s - m).astype(BF16))
            acc = jnp.dot(pr, v_even if par == 0 else v_odd, preferred_element_type=F32)
            den = pltpu.roll(acc, HEAD_DIM, 1) + jnp.exp(sink - m)
            outs.append(acc / den)
        o_ref[:, p * LANES:(p + 1) * LANES] = jnp.where(qlow, outs[0], outs[1]).astype(o_ref.dtype)


def _swa(qkv, sinks, batch, seq):
    t, n = qkv.shape
    heads = sinks.shape[-1]
    nq = heads * HEAD_DIM
    nkv = (n - nq) // 2
    blk = WINDOW
    nb = seq // blk
    kcol = nq // nkv
    return pl.pallas_call(
        functools.partial(_swa_kernel, heads=heads),
        grid=(batch, nb),
        in_specs=[
            pl.BlockSpec((blk, nq), lambda b, i: (b * nb + i, 0)),
            pl.BlockSpec((blk, nkv), lambda b, i: (b * nb + jnp.maximum(i - 1, 0), kcol)),
            pl.BlockSpec((blk, nkv), lambda b, i: (b * nb + i, kcol)),
            pl.BlockSpec((blk, nkv), lambda b, i: (b * nb + jnp.maximum(i - 1, 0), kcol + 1)),
            pl.BlockSpec((blk, nkv), lambda b, i: (b * nb + i, kcol + 1)),
            pl.BlockSpec((1, heads), lambda b, i: (0, 0)),
        ],
        out_specs=pl.BlockSpec((blk, nq), lambda b, i: (b * nb + i, 0)),
        out_shape=jax.ShapeDtypeStruct((t, nq), BF16),
        compiler_params=_params("parallel", "parallel"),
        name="swa_attention",
    )(qkv, qkv, qkv, qkv, qkv, sinks.reshape(1, heads))


def _fox_prep_kernel(gl_ref, f_ref, ft_ref, carry_ref):
    @pl.when(pl.program_id(1) == 0)
    def _():
        carry_ref[...] = jnp.zeros_like(carry_ref)

    ts = gl_ref.shape[0]
    logf = _log_sigmoid(gl_ref[...])
    f = jnp.dot(_tril_f32(ts), logf, preferred_element_type=F32, precision=lax.Precision.HIGHEST) + carry_ref[...]
    f_ref[...] = f
    ft_ref[...] = f.T
    carry_ref[...] = f[ts - 1:ts, :]


def _fox_prep(gate_logits, batch, seq, ts):
    ns = seq // ts
    return pl.pallas_call(
        _fox_prep_kernel,
        grid=(batch, ns),
        in_specs=[pl.BlockSpec((ts, LANES), lambda b, s: (b * ns + s, 0))],
        out_specs=[
            pl.BlockSpec((ts, LANES), lambda b, s: (b * ns + s, 0)),
            pl.BlockSpec((None, None, LANES, ts), lambda b, s: (b, s, 0, 0)),
        ],
        out_shape=[
            jax.ShapeDtypeStruct((batch * seq, LANES), F32),
            jax.ShapeDtypeStruct((batch, ns, LANES, ts), F32),
        ],
        scratch_shapes=[pltpu.VMEM((1, LANES), F32)],
        compiler_params=_params("parallel", "arbitrary"),
        name="fox_gate_cumsum",
    )(gate_logits)


def _fox_kernel(q_ref, k_ref, v_ref, f_ref, ft_ref, o_ref, m_scr, acc_scr, *, tq):
    hp = pl.program_id(1)
    i = pl.program_id(2)
    lane = _lane_iota((tq, LANES))
    low = lane < HEAD_DIM
    q = q_ref[...]
    q_heads = (jnp.where(low, q, jnp.zeros_like(q)), jnp.where(low, jnp.zeros_like(q), q))
    f_tile = f_ref[...]
    contract_lanes = (((1,), (1,)), ((), ()))
    row_in_group = (2 * hp) % SUBLANES

    fq, f0 = [], []
    for par in range(2):
        col = jnp.sum(jnp.where(lane == 2 * hp + par, f_tile, 0.0), axis=-1, keepdims=True)
        f0.append(col[0:1, :])
        fq.append(col - col[0:1, :])

    m_scr[...] = jnp.full(m_scr.shape, NEG, F32)
    acc_scr[...] = jnp.zeros(acc_scr.shape, F32)

    def tile(j, masked):
        start = pl.multiple_of(j * tq, tq)
        k = k_ref[pl.ds(start, tq), :]
        v = v_ref[pl.ds(start, tq), :]
        vlow = _lane_iota(v.shape) < HEAD_DIM
        ones = jnp.ones_like(v)
        v_heads = (jnp.where(vlow, v, ones), jnp.where(vlow, ones, v))
        for par in range(2):
            fk = ft_ref[j, pl.ds(row_in_group + par, 1), :] - f0[par]
            s = lax.dot_general(q_heads[par], k, contract_lanes, preferred_element_type=F32)
            z = s + (fq[par] - fk)
            if masked:
                r = lax.broadcasted_iota(jnp.int32, z.shape, 0)
                c = lax.broadcasted_iota(jnp.int32, z.shape, 1)
                z = jnp.where(c <= r, z, NEG)
            m_prev = m_scr[par]
            m_new = jnp.maximum(m_prev, jnp.max(z, axis=-1, keepdims=True))
            alpha = jnp.exp(m_prev - m_new)
            pr = jnp.exp((z - m_new).astype(BF16))
            acc_scr[par] = alpha * acc_scr[par] + jnp.dot(pr, v_heads[par], preferred_element_type=F32)
            m_scr[par] = m_new

    def body(j, carry):
        tile(j, False)
        return carry

    lax.fori_loop(0, i, body, 0)
    tile(i, True)

    a0 = acc_scr[0]
    a1 = acc_scr[1]
    o_ref[...] = jnp.where(low, a0 / pltpu.roll(a0, HEAD_DIM, 1), a1 / pltpu.roll(a1, HEAD_DIM, 1)).astype(o_ref.dtype)


def _fox(qkv, f, ft, batch, seq, heads):
    t = qkv.shape[0]
    tq = ft.shape[-1]
    nq = seq // tq
    pairs = heads // 2
    return pl.pallas_call(
        functools.partial(_fox_kernel, tq=tq),
        grid=(batch, pairs, nq),
        in_specs=[
            pl.BlockSpec((tq, LANES), lambda b, p, i: (b * nq + i, p)),
            pl.BlockSpec((seq, LANES), lambda b, p, i: (b, pairs + p)),
            pl.BlockSpec((seq, LANES), lambda b, p, i: (b, 2 * pairs + p)),
            pl.BlockSpec((tq, LANES), lambda b, p, i: (b * nq + i, 0)),
            pl.BlockSpec((None, nq, SUBLANES, tq), lambda b, p, i: (b, 0, (2 * p) // SUBLANES, 0)),
        ],
        out_specs=pl.BlockSpec((tq, LANES), lambda b, p, i: (b * nq + i, p)),
        out_shape=jax.ShapeDtypeStruct((t, heads * HEAD_DIM), BF16),
        scratch_shapes=[pltpu.VMEM((2, tq, 1), F32), pltpu.VMEM((2, tq, LANES), F32)],
        compiler_params=_params("parallel", "parallel", "arbitrary"),
        name="fox_attention",
    )(qkv, qkv, qkv, f, ft)


def _mlstm_kernel(q_ref, k_ref, v_ref, o_ref, g_ref, out_ref, c_scr, m_scr, *, dk, dv):
    @pl.when(pl.program_id(1) == 0)
    def _():
        c_scr[...] = jnp.zeros_like(c_scr)
        m_scr[...] = jnp.zeros_like(m_scr)

    L = g_ref.shape[0]
    gates = g_ref[...]
    b_all = jnp.dot(_tril_f32(L), _log_sigmoid(gates), preferred_element_type=F32,
                    precision=lax.Precision.HIGHEST)
    gates_t = gates.T
    b_all_t = b_all.T
    r = lax.broadcasted_iota(jnp.int32, (L, L), 0)
    c = lax.broadcasted_iota(jnp.int32, (L, L), 1)
    tri = c <= r
    ones_col = (_lane_iota((L, LANES)) == 0).astype(BF16)
    contract_lanes = (((1,), (1,)), ((), ()))
    contract_rows = (((0,), (0,)), ((), ()))

    for hd in range(C_HEADS):
        q = q_ref[:, hd * dk:(hd + 1) * dk]
        k = k_ref[:, hd * dk:(hd + 1) * dk]
        v_aug = jnp.concatenate([v_ref[:, hd * dv:(hd + 1) * dv], ones_col], axis=1)
        b_col = b_all[:, C_HEADS + hd:C_HEADS + hd + 1]
        i_col = gates[:, hd:hd + 1]
        b_row = b_all_t[C_HEADS + hd:C_HEADS + hd + 1, :]
        i_row = gates_t[hd:hd + 1, :]
        m_prev = m_scr[hd:hd + 1, 0:1]
        g_last = b_col[L - 1:L, :]

        dm = jnp.where(tri, b_col - b_row + i_row, NEG)
        inter = b_col + m_prev
        m_t = jnp.maximum(inter, jnp.max(dm, axis=-1, keepdims=True))
        a = jnp.exp(dm - m_t) * lax.dot_general(q, k, contract_lanes, preferred_element_type=F32)
        sc = jnp.exp(inter - m_t)
        state = c_scr[hd]
        num = sc * jnp.dot(q, state.astype(BF16), preferred_element_type=F32) \
            + jnp.dot(a.astype(BF16), v_aug, preferred_element_type=F32)
        den = num[:, dv:dv + 1]
        h_t = num[:, :dv] / jnp.maximum(jnp.abs(den), jnp.exp(-m_t))
        gate_o = jax.nn.sigmoid(o_ref[:, hd * dv:(hd + 1) * dv].astype(F32))
        out_ref[:, hd * dv:(hd + 1) * dv] = (gate_o * h_t).astype(out_ref.dtype)

        wlog = g_last - b_col + i_col
        m_new = jnp.maximum(g_last + m_prev, jnp.max(wlog, axis=0, keepdims=True))
        decay = jnp.exp(g_last + m_prev - m_new)
        kw = (k.astype(F32) * jnp.exp(wlog - m_new)).astype(BF16)
        c_scr[hd] = decay * state + lax.dot_general(kw, v_aug, contract_rows, preferred_element_type=F32)
        m_scr[hd:hd + 1, :] = jnp.broadcast_to(m_new, (1, LANES))


def _mlstm(proj, gates, batch, seq):
    t, n = proj.shape
    dv = n // (3 * C_HEADS)
    dk = dv // 2
    L = _pick(seq, (256, 128, 64))
    nc = seq // L
    hq = C_HEADS * dk
    hv = C_HEADS * dv
    qb, vb = hq // hq, (2 * hq) // hv
    return pl.pallas_call(
        functools.partial(_mlstm_kernel, dk=dk, dv=dv),
        grid=(batch, nc),
        in_specs=[
            pl.BlockSpec((L, hq), lambda b, s: (b * nc + s, 0)),
            pl.BlockSpec((L, hq), lambda b, s: (b * nc + s, qb)),
            pl.BlockSpec((L, hv), lambda b, s: (b * nc + s, vb)),
            pl.BlockSpec((L, hv), lambda b, s: (b * nc + s, vb + 1)),
            pl.BlockSpec((L, LANES), lambda b, s: (b * nc + s, 0)),
        ],
        out_specs=pl.BlockSpec((L, hv), lambda b, s: (b * nc + s, 0)),
        out_shape=jax.ShapeDtypeStruct((t, hv), BF16),
        scratch_shapes=[pltpu.VMEM((C_HEADS, dk, dv + LANES), F32), pltpu.VMEM((SUBLANES, LANES), F32)],
        compiler_params=_params("parallel", "arbitrary"),
        name="mlstm",
    )(proj, proj, proj, proj, gates)


def _final_norm_kernel(x_ref, g_ref, o_ref):
    x = x_ref[...]
    o_ref[...] = x * lax.rsqrt(jnp.mean(x * x, axis=-1, keepdims=True) + EPS) * g_ref[...]


def _final_norm(x, g):
    t, d = x.shape
    tm = _pick(t, (1024, 512, 256, 128))
    return pl.pallas_call(
        _final_norm_kernel,
        grid=(t // tm,),
        in_specs=[pl.BlockSpec((tm, d), lambda i: (i, 0)), pl.BlockSpec((1, d), lambda i: (0, 0))],
        out_specs=pl.BlockSpec((tm, d), lambda i: (i, 0)),
        out_shape=jax.ShapeDtypeStruct((t, d), F32),
        compiler_params=_params("parallel"),
        name="final_norm",
    )(x, g)


def _split_gate_cols(w_in, bias, n_main):
    n_gate = w_in.shape[1] - n_main
    wg = jnp.zeros((w_in.shape[0], LANES), BF16).at[:, :n_gate].set(w_in[:, n_main:].astype(BF16))
    bg = jnp.zeros((1, LANES), F32).at[0, :n_gate].set(bias)
    return w_in[:, :n_main].astype(BF16), wg, bg


def kernel(x, c, positions, mod_w, mod_b, norm_g, final_g, a_w_in, a_sinks, a_w_out, b_w_in, b_f_bias, b_w_out, c_w_in, c_gate_bias, c_w_out, ffn_w_up, ffn_conv_w, ffn_conv_b, ffn_w_down):
    batch, seq, d = x.shape
    depth = mod_w.shape[0]
    t = batch * seq
    xf = x.reshape(t, d)

    mod = _modulation(c, mod_w, mod_b)[:, :batch, :].reshape(depth, batch, 6, 1, d)
    cos, sins = _rope_tables(positions)

    for i in range(depth):
        sh1, sc1, g1, sh2, sc2, g2 = (mod[i, :, r] for r in range(6))
        gain1 = norm_g[i, 0].reshape(1, d)
        gain2 = norm_g[i, 1].reshape(1, d)
        kind, j = i % N_MIXERS, i // N_MIXERS
        if kind == 0:
            qkv = _proj_a(xf, gain1, sh1, sc1, a_w_in[j].astype(BF16), cos, sins, seq)
            y = _swa(qkv, a_sinks[j], batch, seq)
            w_out = a_w_out[j]
        elif kind == 1:
            heads = b_f_bias.shape[-1]
            n_main = 3 * heads * HEAD_DIM
            w, wg, bg = _split_gate_cols(b_w_in[j], b_f_bias[j], n_main)
            qkv, gl = _proj_gate(xf, gain1, sh1, sc1, w, wg, bg, seq, heads * HEAD_DIM, HEAD_DIM ** -0.5)
            f, ft = _fox_prep(gl, batch, seq, _pick(seq, (256, 128)))
            y = _fox(qkv, f, ft, batch, seq, heads)
            w_out = b_w_out[j]
        else:
            n_main = c_w_in.shape[-1] - 2 * C_HEADS
            dk = n_main // (6 * C_HEADS)
            w, wg, bg = _split_gate_cols(c_w_in[j], c_gate_bias[j], n_main)
            proj, gl = _proj_gate(xf, gain1, sh1, sc1, w, wg, bg, seq, C_HEADS * dk, dk ** -0.5)
            y = _mlstm(proj, gl, batch, seq)
            w_out = c_w_out[j]
        xf = _proj_res(y, w_out.astype(BF16), xf, g1, seq)

        conv_tab = jnp.zeros((SUBLANES, ffn_conv_w.shape[-1]), F32)
        conv_tab = conv_tab.at[:CONV_WIDTH].set(ffn_conv_w[i]).at[CONV_WIDTH].set(ffn_conv_b[i])
        act = _ffn_up(xf, gain2, sh2, sc2, ffn_w_up[i].astype(BF16), conv_tab, seq)
        xf = _proj_res(act, ffn_w_down[i].astype(BF16), xf, g2, seq)

    return _final_norm(xf, final_g.reshape(1, d)).reshape(batch, seq, d)
```

```python
import functools

import jax
import jax.numpy as jnp
from jax import lax
from jax.experimental import pallas as pl
from jax.experimental.pallas import tpu as pltpu

F32 = jnp.float32
BF16 = jnp.bfloat16

EPS = 1e-6
NEG = -1e30
ROPE_THETA = 10000.0
HEAD_DIM = 64
HALF = HEAD_DIM // 2
GQA_GROUP = 8
WINDOW = 128
C_HEADS = 4
N_MIXERS = 3
CONV_WIDTH = 3
LANES = 128
SUBLANES = 8
LOG2E = 1.4426950408889634
VMEM_LIMIT = 56 * 1024 * 1024


def _pick(dim, prefs):
    for p in prefs:
        if dim % p == 0:
            return p
    return dim


def _params(*sem):
    return pltpu.CompilerParams(dimension_semantics=sem, vmem_limit_bytes=VMEM_LIMIT)


def _lane_iota(shape):
    return lax.broadcasted_iota(jnp.int32, shape, len(shape) - 1)


def _log_sigmoid(x):
    return jnp.minimum(x, 0.0) - jnp.log1p(jnp.exp(-jnp.abs(x)))


def _tril_f32(n):
    r = lax.broadcasted_iota(jnp.int32, (n, n), 0)
    c = lax.broadcasted_iota(jnp.int32, (n, n), 1)
    return (c <= r).astype(F32)


def _mod_kernel(c_ref, w_ref, b_ref, o_ref):
    c = c_ref[...]
    act = (c * jax.nn.sigmoid(c)).astype(BF16)
    o_ref[...] = jnp.dot(act, w_ref[...].astype(BF16), preferred_element_type=F32) + b_ref[...]


def _modulation(c, mod_w, mod_b):
    depth, d, n = mod_w.shape
    b = c.shape[0]
    cp = jnp.zeros((SUBLANES, d), F32).at[:b].set(c)
    tn = _pick(n, (1024, 512, 256, 128))
    return pl.pallas_call(
        _mod_kernel,
        grid=(depth, n // tn),
        in_specs=[
            pl.BlockSpec((SUBLANES, d), lambda l, j: (0, 0)),
            pl.BlockSpec((None, d, tn), lambda l, j: (l, 0, j)),
            pl.BlockSpec((None, 1, tn), lambda l, j: (l, 0, j)),
        ],
        out_specs=pl.BlockSpec((None, SUBLANES, tn), lambda l, j: (l, 0, j)),
        out_shape=jax.ShapeDtypeStruct((depth, SUBLANES, n), F32),
        compiler_params=_params("parallel", "parallel"),
        name="modulation",
    )(cp, mod_w, mod_b.reshape(depth, 1, n))


def _rope_table_kernel(pos_ref, freq_ref, cos_ref, sin_ref):
    ang = pos_ref[...] * freq_ref[...]
    lane = _lane_iota(ang.shape)
    cos_ref[...] = jnp.cos(ang)
    s = jnp.sin(ang)
    sin_ref[...] = jnp.where((lane & (HEAD_DIM - 1)) < HALF, -s, s)


def _rope_tables(positions):
    t = positions.size
    pos = positions.reshape(t, 1).astype(F32)
    inv_freq = ROPE_THETA ** (-jnp.arange(0, HEAD_DIM, 2, dtype=F32) / HEAD_DIM)
    freq = jnp.tile(inv_freq, LANES // HALF).reshape(1, LANES)
    ts = _pick(t, (1024, 512, 256, 128))
    return pl.pallas_call(
        _rope_table_kernel,
        grid=(t // ts,),
        in_specs=[pl.BlockSpec((ts, 1), lambda i: (i, 0)), pl.BlockSpec((1, LANES), lambda i: (0, 0))],
        out_specs=[pl.BlockSpec((ts, LANES), lambda i: (i, 0))] * 2,
        out_shape=[jax.ShapeDtypeStruct((t, LANES), F32)] * 2,
        compiler_params=_params("parallel"),
        name="rope_tables",
    )(pos, freq)


def _norm_mod(x_ref, g_ref, sh_ref, sc_ref):
    x = x_ref[...]
    y = x * lax.rsqrt(jnp.mean(x * x, axis=-1, keepdims=True) + EPS)
    y = y * g_ref[...]
    return (y * (1.0 + sc_ref[...]) + sh_ref[...]).astype(BF16)


def _rope(a, cos, sins):
    lane = _lane_iota(a.shape)
    swapped = jnp.where((lane & (HEAD_DIM - 1)) < HALF, pltpu.roll(a, LANES - HALF, 1), pltpu.roll(a, HALF, 1))
    return a * cos + swapped * sins


def _proj_a_kernel(x_ref, g_ref, sh_ref, sc_ref, w_ref, cos_ref, sin_ref, o_ref, *, nq, nkv):
    h = _norm_mod(x_ref, g_ref, sh_ref, sc_ref)
    acc = jnp.dot(h, w_ref[...], preferred_element_type=F32)
    cos = cos_ref[...]
    sins = sin_ref[...]
    for c in range((nq + nkv) // LANES):
        a = _rope(acc[:, c * LANES:(c + 1) * LANES], cos, sins)
        if c * LANES < nq:
            a = a * (HEAD_DIM ** -0.5)
        o_ref[:, c * LANES:(c + 1) * LANES] = a.astype(BF16)
    o_ref[:, nq + nkv:] = acc[:, nq + nkv:].astype(BF16)


def _proj_a(x, g, sh, sc, w, cos, sins, seq):
    t, d = x.shape
    n = w.shape[1]
    nq = d
    nkv = (n - nq) // 2
    tm = _pick(seq, (512, 256, 128))
    per_b = seq // tm
    return pl.pallas_call(
        functools.partial(_proj_a_kernel, nq=nq, nkv=nkv),
        grid=(t // tm,),
        in_specs=[
            pl.BlockSpec((tm, d), lambda i: (i, 0)),
            pl.BlockSpec((1, d), lambda i: (0, 0)),
            pl.BlockSpec((None, 1, d), lambda i: (i // per_b, 0, 0)),
            pl.BlockSpec((None, 1, d), lambda i: (i // per_b, 0, 0)),
            pl.BlockSpec((d, n), lambda i: (0, 0)),
            pl.BlockSpec((tm, LANES), lambda i: (i, 0)),
            pl.BlockSpec((tm, LANES), lambda i: (i, 0)),
        ],
        out_specs=pl.BlockSpec((tm, n), lambda i: (i, 0)),
        out_shape=jax.ShapeDtypeStruct((t, n), BF16),
        compiler_params=_params("parallel"),
        name="proj_swa",
    )(x, g, sh, sc, w, cos, sins)


def _proj_gate_kernel(x_ref, g_ref, sh_ref, sc_ref, w_ref, wg_ref, bg_ref, o_ref, og_ref, h_scr, *,
                      q_tiles, q_scale):
    j = pl.program_id(1)

    @pl.when(j == 0)
    def _():
        h_scr[...] = _norm_mod(x_ref, g_ref, sh_ref, sc_ref)
        og_ref[...] = jnp.dot(h_scr[...], wg_ref[...], preferred_element_type=F32) + bg_ref[...]

    acc = jnp.dot(h_scr[...], w_ref[...], preferred_element_type=F32)
    o_ref[...] = (acc * jnp.where(j < q_tiles, q_scale, 1.0)).astype(BF16)


def _proj_gate(x, g, sh, sc, w, wg, bg, seq, q_cols, q_scale):
    t, d = x.shape
    n = w.shape[1]
    tm = _pick(seq, (1024, 512, 256, 128))
    tn = _pick(q_cols, (1024, 512, 256, 128))
    per_b = seq // tm
    return pl.pallas_call(
        functools.partial(_proj_gate_kernel, q_tiles=q_cols // tn, q_scale=q_scale),
        grid=(t // tm, n // tn),
        in_specs=[
            pl.BlockSpec((tm, d), lambda i, j: (i, 0)),
            pl.BlockSpec((1, d), lambda i, j: (0, 0)),
            pl.BlockSpec((None, 1, d), lambda i, j: (i // per_b, 0, 0)),
            pl.BlockSpec((None, 1, d), lambda i, j: (i // per_b, 0, 0)),
            pl.BlockSpec((d, tn), lambda i, j: (0, j)),
            pl.BlockSpec((d, LANES), lambda i, j: (0, 0)),
            pl.BlockSpec((1, LANES), lambda i, j: (0, 0)),
        ],
        out_specs=[pl.BlockSpec((tm, tn), lambda i, j: (i, j)), pl.BlockSpec((tm, LANES), lambda i, j: (i, 0))],
        out_shape=[jax.ShapeDtypeStruct((t, n), BF16), jax.ShapeDtypeStruct((t, LANES), F32)],
        scratch_shapes=[pltpu.VMEM((tm, d), BF16)],
        compiler_params=_params("parallel", "arbitrary"),
        name="proj_gate",
    )(x, g, sh, sc, w, wg, bg)


def _proj_res_kernel(y_ref, w_ref, x_ref, g_ref, o_ref):
    acc = jnp.dot(y_ref[...], w_ref[...], preferred_element_type=F32)
    o_ref[...] = x_ref[...] + g_ref[...] * acc


def _proj_res(y, w, x, gate, seq):
    t, k = y.shape
    n = w.shape[1]
    tm = _pick(seq, (1024, 512, 256, 128))
    tn = _pick(n, (512, 256, 128))
    per_b = seq // tm
    return pl.pallas_call(
        _proj_res_kernel,
        grid=(t // tm, n // tn),
        in_specs=[
            pl.BlockSpec((tm, k), lambda i, j: (i, 0)),
            pl.BlockSpec((k, tn), lambda i, j: (0, j)),
            pl.BlockSpec((tm, tn), lambda i, j: (i, j)),
            pl.BlockSpec((None, 1, tn), lambda i, j: (i // per_b, 0, j)),
        ],
        out_specs=pl.BlockSpec((tm, tn), lambda i, j: (i, j)),
        out_shape=jax.ShapeDtypeStruct((t, n), F32),
        input_output_aliases={2: 0},
        compiler_params=_params("parallel", "arbitrary"),
        name="proj_residual",
    )(y, w, x, gate)


def _ffn_up_kernel(x_ref, g_ref, sh_ref, sc_ref, wg_ref, wv_ref, cg_ref, cv_ref, o_ref, h_scr, tg_scr, tv_scr, *,
                   per_b):
    i = pl.program_id(0)
    j = pl.program_id(1)
    tm = o_ref.shape[0]

    @pl.when(j == 0)
    def _():
        h_scr[...] = _norm_mod(x_ref, g_ref, sh_ref, sc_ref)

    @pl.when(i == 0)
    def _():
        tg_scr[j] = jnp.zeros(tg_scr.shape[1:], F32)
        tv_scr[j] = jnp.zeros(tv_scr.shape[1:], F32)

    h = h_scr[...]
    ug = jnp.dot(h, wg_ref[...], preferred_element_type=F32)
    uv = jnp.dot(h, wv_ref[...], preferred_element_type=F32)

    def conv(u, c_ref, shifted1, shifted2):
        return u * c_ref[2:3, :] + shifted1 * c_ref[1:2, :] + shifted2 * c_ref[0:1, :] + c_ref[3:4, :]

    def glu(yg, yv):
        return (yg * jax.nn.sigmoid(yg) * yv).astype(o_ref.dtype)

    yg = conv(ug, cg_ref, pltpu.roll(ug, 1, 0), pltpu.roll(ug, 2, 0))
    yv = conv(uv, cv_ref, pltpu.roll(uv, 1, 0), pltpu.roll(uv, 2, 0))
    o_ref[...] = glu(yg, yv)

    first = (i % per_b) == 0
    row = lax.broadcasted_iota(jnp.int32, (SUBLANES, ug.shape[1]), 0)

    def head_rows(u, t_scr, c_ref):
        top = u[0:SUBLANES, :]
        tail = jnp.where(first, 0.0, t_scr[j])
        s1 = jnp.where(row == 0, pltpu.roll(tail, 1, 0), pltpu.roll(top, 1, 0))
        s2 = jnp.where(row <= 1, pltpu.roll(tail, 2, 0), pltpu.roll(top, 2, 0))
        t_scr[j] = u[tm - SUBLANES:tm, :]
        return conv(top, c_ref, s1, s2)

    o_ref[0:SUBLANES, :] = glu(head_rows(ug, tg_scr, cg_ref), head_rows(uv, tv_scr, cv_ref))


def _ffn_up(x, g, sh, sc, w_up, conv_tab, seq):
    t, d = x.shape
    f = w_up.shape[1] // 2
    tm = _pick(seq, (1024, 512, 256, 128))
    tn = _pick(f, (512, 256, 128))
    nj = f // tn
    per_b = seq // tm
    return pl.pallas_call(
        functools.partial(_ffn_up_kernel, per_b=per_b),
        grid=(t // tm, nj),
        in_specs=[
            pl.BlockSpec((tm, d), lambda i, j: (i, 0)),
            pl.BlockSpec((1, d), lambda i, j: (0, 0)),
            pl.BlockSpec((None, 1, d), lambda i, j: (i // per_b, 0, 0)),
            pl.BlockSpec((None, 1, d), lambda i, j: (i // per_b, 0, 0)),
            pl.BlockSpec((d, tn), lambda i, j: (0, j)),
            pl.BlockSpec((d, tn), lambda i, j: (0, j + nj)),
            pl.BlockSpec((SUBLANES, tn), lambda i, j: (0, j)),
            pl.BlockSpec((SUBLANES, tn), lambda i, j: (0, j + nj)),
        ],
        out_specs=pl.BlockSpec((tm, tn), lambda i, j: (i, j)),
        out_shape=jax.ShapeDtypeStruct((t, f), BF16),
        scratch_shapes=[
            pltpu.VMEM((tm, d), BF16),
            pltpu.VMEM((nj, SUBLANES, tn), F32),
            pltpu.VMEM((nj, SUBLANES, tn), F32),
        ],
        compiler_params=_params("arbitrary", "arbitrary"),
        name="ffn_up_conv_glu",
    )(x, g, sh, sc, w_up, w_up, conv_tab, conv_tab)


def _swa_kernel(q_ref, kp_ref, kc_ref, vp_ref, vc_ref, sink_ref, o_ref, s_scr, p_scr, e_scr, *, heads):
    n = pl.program_id(1)
    blk = q_ref.shape[0]
    kb = jnp.concatenate([kp_ref[...], kc_ref[...]], axis=0).astype(F32)
    vb = jnp.concatenate([vp_ref[...], vc_ref[...]], axis=0).astype(F32)
    lane = _lane_iota((2 * blk, LANES))
    low = lane < HEAD_DIM
    qi = lax.broadcasted_iota(jnp.int32, (blk, 2 * blk), 0)
    kj = lax.broadcasted_iota(jnp.int32, (blk, 2 * blk), 1)
    rel = qi + blk - kj
    first_valid_key = jnp.where(n > 0, 0, blk)
    mask = (rel >= 0) & (rel < WINDOW) & (kj >= first_valid_key)
    qlow = _lane_iota((blk, LANES)) < HEAD_DIM
    contract_lanes = (((1,), (1,)), ((), ()))

    kv_cache = {}

    def kv_group(g):
        if g not in kv_cache:
            c, half = divmod(g, 2)
            kc = kb[:, c * LANES:(c + 1) * LANES]
            vc = vb[:, c * LANES:(c + 1) * LANES]
            kr = pltpu.roll(kc, HEAD_DIM, 1)
            vr = pltpu.roll(vc, HEAD_DIM, 1)
            k2 = (jnp.where(low, kc, kr) if half == 0 else jnp.where(low, kr, kc)).astype(BF16)
            v_lo, v_hi = (vc, vr) if half == 0 else (vr, vc)
            v_even = jnp.where(low, v_lo, 1.0).astype(BF16)
            v_odd = jnp.where(low, 1.0, v_hi).astype(BF16)
            kv_cache[g] = (k2, v_even, v_odd)
        return kv_cache[g]

    for p in range(heads // 2):
        k2 = kv_group((2 * p) // GQA_GROUP)[0]
        qp = q_ref[:, p * LANES:(p + 1) * LANES]
        for par in range(2):
            qh = jnp.where(qlow if par == 0 else ~qlow, qp, jnp.zeros_like(qp))
            s_scr[2 * p + par] = lax.dot_general(qh, k2, contract_lanes, preferred_element_type=F32)

    s = jnp.where(mask[None], s_scr[...], NEG)
    sink = sink_ref[...]
    m = jnp.maximum(jnp.max(s, axis=-1, keepdims=True), sink)
    p_scr[...] = jnp.exp((s - m).astype(BF16))
    e_scr[...] = jnp.broadcast_to(jnp.exp(sink - m), e_scr.shape)

    for p in range(heads // 2):
        _, v_even, v_odd = kv_group((2 * p) // GQA_GROUP)
        outs = []
        for par in range(2):
            hq = 2 * p + par
            acc = jnp.dot(p_scr[hq], v_even if par == 0 else v_odd, preferred_element_type=F32)
            outs.append(acc / (pltpu.roll(acc, HEAD_DIM, 1) + e_scr[hq]))
        o_ref[:, p * LANES:(p + 1) * LANES] = jnp.where(qlow, outs[0], outs[1]).astype(o_ref.dtype)


def _swa(qkv, sinks, batch, seq):
    t, n = qkv.shape
    heads = sinks.shape[-1]
    nq = heads * HEAD_DIM
    nkv = (n - nq) // 2
    blk = WINDOW
    nb = seq // blk
    kcol = nq // nkv
    return pl.pallas_call(
        functools.partial(_swa_kernel, heads=heads),
        grid=(batch, nb),
        in_specs=[
            pl.BlockSpec((blk, nq), lambda b, i: (b * nb + i, 0)),
            pl.BlockSpec((blk, nkv), lambda b, i: (b * nb + jnp.maximum(i - 1, 0), kcol)),
            pl.BlockSpec((blk, nkv), lambda b, i: (b * nb + i, kcol)),
            pl.BlockSpec((blk, nkv), lambda b, i: (b * nb + jnp.maximum(i - 1, 0), kcol + 1)),
            pl.BlockSpec((blk, nkv), lambda b, i: (b * nb + i, kcol + 1)),
            pl.BlockSpec((heads, 1, 1), lambda b, i: (0, 0, 0)),
        ],
        out_specs=pl.BlockSpec((blk, nq), lambda b, i: (b * nb + i, 0)),
        out_shape=jax.ShapeDtypeStruct((t, nq), BF16),
        scratch_shapes=[
            pltpu.VMEM((heads, blk, 2 * blk), F32),
            pltpu.VMEM((heads, blk, 2 * blk), BF16),
            pltpu.VMEM((heads, blk, LANES), F32),
        ],
        compiler_params=_params("parallel", "parallel"),
        name="swa_attention",
    )(qkv, qkv, qkv, qkv, qkv, sinks.reshape(heads, 1, 1))


def _fox_prep_kernel(gl_ref, f_ref, ft_ref, carry_ref):
    @pl.when(pl.program_id(1) == 0)
    def _():
        carry_ref[...] = jnp.zeros_like(carry_ref)

    ts = gl_ref.shape[0]
    logf = _log_sigmoid(gl_ref[...]) * LOG2E
    f = jnp.dot(_tril_f32(ts), logf, preferred_element_type=F32, precision=lax.Precision.HIGHEST) + carry_ref[...]
    f_ref[...] = f
    ft_ref[...] = f.T
    carry_ref[...] = f[ts - 1:ts, :]


def _fox_prep(gate_logits, batch, seq, ts):
    ns = seq // ts
    return pl.pallas_call(
        _fox_prep_kernel,
        grid=(batch, ns),
        in_specs=[pl.BlockSpec((ts, LANES), lambda b, s: (b * ns + s, 0))],
        out_specs=[
            pl.BlockSpec((ts, LANES), lambda b, s: (b * ns + s, 0)),
            pl.BlockSpec((None, None, LANES, ts), lambda b, s: (b, s, 0, 0)),
        ],
        out_shape=[
            jax.ShapeDtypeStruct((batch * seq, LANES), F32),
            jax.ShapeDtypeStruct((batch, ns, LANES, ts), F32),
        ],
        scratch_shapes=[pltpu.VMEM((1, LANES), F32)],
        compiler_params=_params("parallel", "arbitrary"),
        name="fox_gate_cumsum",
    )(gate_logits)


def _fox_kernel(q_ref, k_ref, v_ref, f_ref, ft_ref, o_ref, m_scr, alpha_scr, acc_scr, s_scr, p_scr, *, tq, tk):
    hp = pl.program_id(1)
    i = pl.program_id(2)
    lane = _lane_iota((tq, LANES))
    low = lane < HEAD_DIM
    q = q_ref[...]
    q_heads = (jnp.where(low, q, jnp.zeros_like(q)), jnp.where(low, jnp.zeros_like(q), q))
    f_tile = f_ref[...]
    contract_lanes = (((1,), (1,)), ((), ()))
    row_in_group = (2 * hp) % SUBLANES

    fq, f0 = [], []
    for par in range(2):
        col = jnp.sum(jnp.where(lane == 2 * hp + par, f_tile, 0.0), axis=-1, keepdims=True)
        f0.append(col[0:1, :])
        fq.append(col - col[0:1, :])

    m_scr[...] = jnp.full(m_scr.shape, NEG, F32)
    acc_scr[...] = jnp.zeros(acc_scr.shape, F32)
    n_full = (i * tq) // tk

    def tile(j, masked):
        start = pl.multiple_of(j * tk, tk)
        k = k_ref[pl.ds(start, tk), :]
        v = v_ref[pl.ds(start, tk), :]
        vlow = _lane_iota(v.shape) < HEAD_DIM
        ones = jnp.ones_like(v)
        v_heads = (jnp.where(vlow, v, ones), jnp.where(vlow, ones, v))
        for par in range(2):
            s_scr[par] = lax.dot_general(q_heads[par], k, contract_lanes, preferred_element_type=F32)
        for par in range(2):
            fk = ft_ref[j, pl.ds(row_in_group + par, 1), :] - f0[par]
            z = s_scr[par] + (fq[par] - fk)
            if masked:
                r = lax.broadcasted_iota(jnp.int32, z.shape, 0) + i * tq
                c = lax.broadcasted_iota(jnp.int32, z.shape, 1) + start
                z = jnp.where(c <= r, z, NEG)
            m_prev = m_scr[par]
            m_new = jnp.maximum(m_prev, jnp.max(z, axis=-1, keepdims=True))
            alpha_scr[par] = jnp.exp2(m_prev - m_new)
            m_scr[par] = m_new
            p_scr[par] = jnp.exp2((z - jnp.concatenate([m_new] * (tk // LANES), axis=1)).astype(BF16))
        for par in range(2):
            acc_scr[par] = alpha_scr[par] * acc_scr[par] + jnp.dot(p_scr[par], v_heads[par],
                                                                   preferred_element_type=F32)

    def body(j, carry):
        tile(j, False)
        return carry

    lax.fori_loop(0, n_full, body, 0)
    tile(n_full, True)

    a0 = acc_scr[0]
    a1 = acc_scr[1]
    o_ref[...] = jnp.where(low, a0 / pltpu.roll(a0, HEAD_DIM, 1), a1 / pltpu.roll(a1, HEAD_DIM, 1)).astype(o_ref.dtype)


def _fox(qkv, f, ft, batch, seq, heads):
    t = qkv.shape[0]
    tk = ft.shape[-1]
    tq = _pick(seq, (256, 128))
    nq = seq // tq
    nk = seq // tk
    pairs = heads // 2
    return pl.pallas_call(
        functools.partial(_fox_kernel, tq=tq, tk=tk),
        grid=(batch, pairs, nq),
        in_specs=[
            pl.BlockSpec((tq, LANES), lambda b, p, i: (b * nq + i, p)),
            pl.BlockSpec((seq, LANES), lambda b, p, i: (b, pairs + p)),
            pl.BlockSpec((seq, LANES), lambda b, p, i: (b, 2 * pairs + p)),
            pl.BlockSpec((tq, LANES), lambda b, p, i: (b * nq + i, 0)),
            pl.BlockSpec((None, nk, SUBLANES, tk), lambda b, p, i: (b, 0, (2 * p) // SUBLANES, 0)),
        ],
        out_specs=pl.BlockSpec((tq, LANES), lambda b, p, i: (b * nq + i, p)),
        out_shape=jax.ShapeDtypeStruct((t, heads * HEAD_DIM), BF16),
        scratch_shapes=[
            pltpu.VMEM((2, tq, LANES), F32),
            pltpu.VMEM((2, tq, LANES), F32),
            pltpu.VMEM((2, tq, LANES), F32),
            pltpu.VMEM((2, tq, tk), F32),
            pltpu.VMEM((2, tq, tk), BF16),
        ],
        compiler_params=_params("parallel", "parallel", "arbitrary"),
        name="fox_attention",
    )(qkv, qkv, qkv, f, ft)


def _mlstm_kernel(q_ref, k_ref, v_ref, o_ref, g_ref, out_ref, c_scr, m_scr, *, dk, dv):
    @pl.when(pl.program_id(1) == 0)
    def _():
        c_scr[...] = jnp.zeros_like(c_scr)
        m_scr[...] = jnp.zeros_like(m_scr)

    L = g_ref.shape[0]
    gates = g_ref[...]
    b_all = jnp.dot(_tril_f32(L), _log_sigmoid(gates), preferred_element_type=F32,
                    precision=lax.Precision.HIGHEST)
    gates_t = gates.T
    b_all_t = b_all.T
    r = lax.broadcasted_iota(jnp.int32, (L, L), 0)
    c = lax.broadcasted_iota(jnp.int32, (L, L), 1)
    tri = c <= r
    ones_col = (_lane_iota((L, LANES)) == 0).astype(BF16)
    contract_lanes = (((1,), (1,)), ((), ()))
    contract_rows = (((0,), (0,)), ((), ()))

    for hd in range(C_HEADS):
        q = q_ref[:, hd * dk:(hd + 1) * dk]
        k = k_ref[:, hd * dk:(hd + 1) * dk]
        v_aug = jnp.concatenate([v_ref[:, hd * dv:(hd + 1) * dv], ones_col], axis=1)
        b_col = b_all[:, C_HEADS + hd:C_HEADS + hd + 1]
        i_col = gates[:, hd:hd + 1]
        b_row = b_all_t[C_HEADS + hd:C_HEADS + hd + 1, :]
        i_row = gates_t[hd:hd + 1, :]
        m_prev = m_scr[hd:hd + 1, 0:1]
        g_last = b_col[L - 1:L, :]

        dm = jnp.where(tri, b_col - b_row + i_row, NEG)
        inter = b_col + m_prev
        m_t = jnp.maximum(inter, jnp.max(dm, axis=-1, keepdims=True))
        a = jnp.exp(dm - m_t) * lax.dot_general(q, k, contract_lanes, preferred_element_type=F32)
        sc = jnp.exp(inter - m_t)
        state = c_scr[hd]
        num = sc * jnp.dot(q, state.astype(BF16), preferred_element_type=F32) \
            + jnp.dot(a.astype(BF16), v_aug, preferred_element_type=F32)
        den = num[:, dv:dv + 1]
        h_t = num[:, :dv] / jnp.maximum(jnp.abs(den), jnp.exp(-m_t))
        gate_o = jax.nn.sigmoid(o_ref[:, hd * dv:(hd + 1) * dv].astype(F32))
        out_ref[:, hd * dv:(hd + 1) * dv] = (gate_o * h_t).astype(out_ref.dtype)

        wlog = g_last - b_col + i_col
        m_new = jnp.maximum(g_last + m_prev, jnp.max(wlog, axis=0, keepdims=True))
        decay = jnp.exp(g_last + m_prev - m_new)
        kw = (k.astype(F32) * jnp.exp(wlog - m_new)).astype(BF16)
        c_scr[hd] = decay * state + lax.dot_general(kw, v_aug, contract_rows, preferred_element_type=F32)
        m_scr[hd:hd + 1, :] = jnp.broadcast_to(m_new, (1, LANES))


def _mlstm(proj, gates, batch, seq):
    t, n = proj.shape
    dv = n // (3 * C_HEADS)
    dk = dv // 2
    L = _pick(seq, (256, 128, 64))
    nc = seq // L
    hq = C_HEADS * dk
    hv = C_HEADS * dv
    qb, vb = hq // hq, (2 * hq) // hv
    return pl.pallas_call(
        functools.partial(_mlstm_kernel, dk=dk, dv=dv),
        grid=(batch, nc),
        in_specs=[
            pl.BlockSpec((L, hq), lambda b, s: (b * nc + s, 0)),
            pl.BlockSpec((L, hq), lambda b, s: (b * nc + s, qb)),
            pl.BlockSpec((L, hv), lambda b, s: (b * nc + s, vb)),
            pl.BlockSpec((L, hv), lambda b, s: (b * nc + s, vb + 1)),
            pl.BlockSpec((L, LANES), lambda b, s: (b * nc + s, 0)),
        ],
        out_specs=pl.BlockSpec((L, hv), lambda b, s: (b * nc + s, 0)),
        out_shape=jax.ShapeDtypeStruct((t, hv), BF16),
        scratch_shapes=[pltpu.VMEM((C_HEADS, dk, dv + LANES), F32), pltpu.VMEM((SUBLANES, LANES), F32)],
        compiler_params=_params("parallel", "arbitrary"),
        name="mlstm",
    )(proj, proj, proj, proj, gates)


def _final_norm_kernel(x_ref, g_ref, o_ref):
    x = x_ref[...]
    o_ref[...] = x * lax.rsqrt(jnp.mean(x * x, axis=-1, keepdims=True) + EPS) * g_ref[...]


def _final_norm(x, g):
    t, d = x.shape
    tm = _pick(t, (1024, 512, 256, 128))
    return pl.pallas_call(
        _final_norm_kernel,
        grid=(t // tm,),
        in_specs=[pl.BlockSpec((tm, d), lambda i: (i, 0)), pl.BlockSpec((1, d), lambda i: (0, 0))],
        out_specs=pl.BlockSpec((tm, d), lambda i: (i, 0)),
        out_shape=jax.ShapeDtypeStruct((t, d), F32),
        compiler_params=_params("parallel"),
        name="final_norm",
    )(x, g)


def _split_gate_cols(w_in, bias, n_main):
    n_gate = w_in.shape[1] - n_main
    wg = jnp.zeros((w_in.shape[0], LANES), BF16).at[:, :n_gate].set(w_in[:, n_main:].astype(BF16))
    bg = jnp.zeros((1, LANES), F32).at[0, :n_gate].set(bias)
    return w_in[:, :n_main].astype(BF16), wg, bg


def kernel(x, c, positions, mod_w, mod_b, norm_g, final_g, a_w_in, a_sinks, a_w_out, b_w_in, b_f_bias, b_w_out, c_w_in, c_gate_bias, c_w_out, ffn_w_up, ffn_conv_w, ffn_conv_b, ffn_w_down):
    batch, seq, d = x.shape
    depth = mod_w.shape[0]
    t = batch * seq
    xf = x.reshape(t, d)

    mod = _modulation(c, mod_w, mod_b)[:, :batch, :].reshape(depth, batch, 6, 1, d)
    cos, sins = _rope_tables(positions)

    for i in range(depth):
        sh1, sc1, g1, sh2, sc2, g2 = (mod[i, :, r] for r in range(6))
        gain1 = norm_g[i, 0].reshape(1, d)
        gain2 = norm_g[i, 1].reshape(1, d)
        kind, j = i % N_MIXERS, i // N_MIXERS
        if kind == 0:
            qkv = _proj_a(xf, gain1, sh1, sc1, a_w_in[j].astype(BF16), cos, sins, seq)
            y = _swa(qkv, a_sinks[j], batch, seq)
            w_out = a_w_out[j]
        elif kind == 1:
            heads = b_f_bias.shape[-1]
            n_main = 3 * heads * HEAD_DIM
            w, wg, bg = _split_gate_cols(b_w_in[j], b_f_bias[j], n_main)
            qkv, gl = _proj_gate(xf, gain1, sh1, sc1, w, wg, bg, seq, heads * HEAD_DIM, HEAD_DIM ** -0.5 * LOG2E)
            f, ft = _fox_prep(gl, batch, seq, _pick(seq, (512, 256, 128)))
            y = _fox(qkv, f, ft, batch, seq, heads)
            w_out = b_w_out[j]
        else:
            n_main = c_w_in.shape[-1] - 2 * C_HEADS
            dk = n_main // (6 * C_HEADS)
            w, wg, bg = _split_gate_cols(c_w_in[j], c_gate_bias[j], n_main)
            proj, gl = _proj_gate(xf, gain1, sh1, sc1, w, wg, bg, seq, C_HEADS * dk, dk ** -0.5)
            y = _mlstm(proj, gl, batch, seq)
            w_out = c_w_out[j]
        xf = _proj_res(y, w_out.astype(BF16), xf, g1, seq)

        conv_tab = jnp.zeros((SUBLANES, ffn_conv_w.shape[-1]), F32)
        conv_tab = conv_tab.at[:CONV_WIDTH].set(ffn_conv_w[i]).at[CONV_WIDTH].set(ffn_conv_b[i])
        act = _ffn_up(xf, gain2, sh2, sc2, ffn_w_up[i].astype(BF16), conv_tab, seq)
        xf = _proj_res(act, ffn_w_down[i].astype(BF16), xf, g2, seq)

    return _final_norm(xf, final_g.reshape(1, d)).reshape(batch, seq, d)
```

```python
import functools

import jax
import jax.numpy as jnp
from jax import lax
from jax.experimental import pallas as pl
from jax.experimental.pallas import tpu as pltpu

F32 = jnp.float32
BF16 = jnp.bfloat16

EPS = 1e-6
NEG = -1e30
ROPE_THETA = 10000.0
HEAD_DIM = 64
HALF = HEAD_DIM // 2
GQA_GROUP = 8
WINDOW = 128
C_HEADS = 4
N_MIXERS = 3
CONV_WIDTH = 3
LANES = 128
SUBLANES = 8
LOG2E = 1.4426950408889634
VMEM_LIMIT = 56 * 1024 * 1024


def _pick(dim, prefs):
    for p in prefs:
        if dim % p == 0:
            return p
    return dim


def _params(*sem):
    return pltpu.CompilerParams(dimension_semantics=sem, vmem_limit_bytes=VMEM_LIMIT)


def _lane_iota(shape):
    return lax.broadcasted_iota(jnp.int32, shape, len(shape) - 1)


def _log_sigmoid(x):
    return jnp.minimum(x, 0.0) - jnp.log1p(jnp.exp(-jnp.abs(x)))


def _tril_f32(n):
    r = lax.broadcasted_iota(jnp.int32, (n, n), 0)
    c = lax.broadcasted_iota(jnp.int32, (n, n), 1)
    return (c <= r).astype(F32)


def _mod_kernel(c_ref, w_ref, b_ref, o_ref):
    c = c_ref[...]
    act = (c * jax.nn.sigmoid(c)).astype(BF16)
    o_ref[...] = jnp.dot(act, w_ref[...].astype(BF16), preferred_element_type=F32) + b_ref[...]


def _modulation(c, mod_w, mod_b):
    depth, d, n = mod_w.shape
    b = c.shape[0]
    cp = jnp.zeros((SUBLANES, d), F32).at[:b].set(c)
    tn = _pick(n, (1024, 512, 256, 128))
    return pl.pallas_call(
        _mod_kernel,
        grid=(depth, n // tn),
        in_specs=[
            pl.BlockSpec((SUBLANES, d), lambda l, j: (0, 0)),
            pl.BlockSpec((None, d, tn), lambda l, j: (l, 0, j)),
            pl.BlockSpec((None, 1, tn), lambda l, j: (l, 0, j)),
        ],
        out_specs=pl.BlockSpec((None, SUBLANES, tn), lambda l, j: (l, 0, j)),
        out_shape=jax.ShapeDtypeStruct((depth, SUBLANES, n), F32),
        compiler_params=_params("parallel", "parallel"),
        name="modulation",
    )(cp, mod_w, mod_b.reshape(depth, 1, n))


def _rope_table_kernel(pos_ref, freq_ref, cos_ref, sin_ref):
    ang = pos_ref[...] * freq_ref[...]
    lane = _lane_iota(ang.shape)
    cos_ref[...] = jnp.cos(ang)
    s = jnp.sin(ang)
    sin_ref[...] = jnp.where((lane & (HEAD_DIM - 1)) < HALF, -s, s)


def _rope_tables(positions):
    t = positions.size
    pos = positions.reshape(t, 1).astype(F32)
    inv_freq = ROPE_THETA ** (-jnp.arange(0, HEAD_DIM, 2, dtype=F32) / HEAD_DIM)
    freq = jnp.tile(inv_freq, LANES // HALF).reshape(1, LANES)
    ts = _pick(t, (1024, 512, 256, 128))
    return pl.pallas_call(
        _rope_table_kernel,
        grid=(t // ts,),
        in_specs=[pl.BlockSpec((ts, 1), lambda i: (i, 0)), pl.BlockSpec((1, LANES), lambda i: (0, 0))],
        out_specs=[pl.BlockSpec((ts, LANES), lambda i: (i, 0))] * 2,
        out_shape=[jax.ShapeDtypeStruct((t, LANES), F32)] * 2,
        compiler_params=_params("parallel"),
        name="rope_tables",
    )(pos, freq)


def _norm_mod(x_ref, g_ref, sh_ref, sc_ref):
    x = x_ref[...]
    y = x * lax.rsqrt(jnp.mean(x * x, axis=-1, keepdims=True) + EPS)
    y = y * g_ref[...]
    return (y * (1.0 + sc_ref[...]) + sh_ref[...]).astype(BF16)


def _rope(a, cos, sins):
    lane = _lane_iota(a.shape)
    swapped = jnp.where((lane & (HEAD_DIM - 1)) < HALF, pltpu.roll(a, LANES - HALF, 1), pltpu.roll(a, HALF, 1))
    return a * cos + swapped * sins


def _proj_a_kernel(x_ref, g_ref, sh_ref, sc_ref, w_ref, cos_ref, sin_ref, o_ref, *, nq, nkv):
    h = _norm_mod(x_ref, g_ref, sh_ref, sc_ref)
    acc = jnp.dot(h, w_ref[...], preferred_element_type=F32)
    cos = cos_ref[...]
    sins = sin_ref[...]
    for c in range((nq + nkv) // LANES):
        a = _rope(acc[:, c * LANES:(c + 1) * LANES], cos, sins)
        if c * LANES < nq:
            a = a * (HEAD_DIM ** -0.5)
        o_ref[:, c * LANES:(c + 1) * LANES] = a.astype(BF16)
    o_ref[:, nq + nkv:] = acc[:, nq + nkv:].astype(BF16)


def _proj_a(x, g, sh, sc, w, cos, sins, seq):
    t, d = x.shape
    n = w.shape[1]
    nq = d
    nkv = (n - nq) // 2
    tm = _pick(seq, (512, 256, 128))
    per_b = seq // tm
    return pl.pallas_call(
        functools.partial(_proj_a_kernel, nq=nq, nkv=nkv),
        grid=(t // tm,),
        in_specs=[
            pl.BlockSpec((tm, d), lambda i: (i, 0)),
            pl.BlockSpec((1, d), lambda i: (0, 0)),
            pl.BlockSpec((None, 1, d), lambda i: (i // per_b, 0, 0)),
            pl.BlockSpec((None, 1, d), lambda i: (i // per_b, 0, 0)),
            pl.BlockSpec((d, n), lambda i: (0, 0)),
            pl.BlockSpec((tm, LANES), lambda i: (i, 0)),
            pl.BlockSpec((tm, LANES), lambda i: (i, 0)),
        ],
        out_specs=pl.BlockSpec((tm, n), lambda i: (i, 0)),
        out_shape=jax.ShapeDtypeStruct((t, n), BF16),
        compiler_params=_params("parallel"),
        name="proj_swa",
    )(x, g, sh, sc, w, cos, sins)


def _proj_gate_kernel(x_ref, g_ref, sh_ref, sc_ref, w_ref, wg_ref, bg_ref, o_ref, og_ref, h_scr, *,
                      q_tiles, q_scale):
    j = pl.program_id(1)

    @pl.when(j == 0)
    def _():
        h_scr[...] = _norm_mod(x_ref, g_ref, sh_ref, sc_ref)
        og_ref[...] = jnp.dot(h_scr[...], wg_ref[...], preferred_element_type=F32) + bg_ref[...]

    acc = jnp.dot(h_scr[...], w_ref[...], preferred_element_type=F32)
    o_ref[...] = (acc * jnp.where(j < q_tiles, q_scale, 1.0)).astype(BF16)


def _proj_gate(x, g, sh, sc, w, wg, bg, seq, q_cols, q_scale):
    t, d = x.shape
    n = w.shape[1]
    tm = _pick(seq, (1024, 512, 256, 128))
    tn = _pick(q_cols, (1024, 512, 256, 128))
    per_b = seq // tm
    return pl.pallas_call(
        functools.partial(_proj_gate_kernel, q_tiles=q_cols // tn, q_scale=q_scale),
        grid=(t // tm, n // tn),
        in_specs=[
            pl.BlockSpec((tm, d), lambda i, j: (i, 0)),
            pl.BlockSpec((1, d), lambda i, j: (0, 0)),
            pl.BlockSpec((None, 1, d), lambda i, j: (i // per_b, 0, 0)),
            pl.BlockSpec((None, 1, d), lambda i, j: (i // per_b, 0, 0)),
            pl.BlockSpec((d, tn), lambda i, j: (0, j)),
            pl.BlockSpec((d, LANES), lambda i, j: (0, 0)),
            pl.BlockSpec((1, LANES), lambda i, j: (0, 0)),
        ],
        out_specs=[pl.BlockSpec((tm, tn), lambda i, j: (i, j)), pl.BlockSpec((tm, LANES), lambda i, j: (i, 0))],
        out_shape=[jax.ShapeDtypeStruct((t, n), BF16), jax.ShapeDtypeStruct((t, LANES), F32)],
        scratch_shapes=[pltpu.VMEM((tm, d), BF16)],
        compiler_params=_params("parallel", "arbitrary"),
        name="proj_gate",
    )(x, g, sh, sc, w, wg, bg)


def _proj_res_kernel(y_ref, w_ref, x_ref, g_ref, o_ref):
    acc = jnp.dot(y_ref[...], w_ref[...], preferred_element_type=F32)
    o_ref[...] = x_ref[...] + g_ref[...] * acc


def _proj_res(y, w, x, gate, seq):
    t, k = y.shape
    n = w.shape[1]
    tm = _pick(seq, (1024, 512, 256, 128))
    tn = _pick(n, (512, 256, 128))
    per_b = seq // tm
    return pl.pallas_call(
        _proj_res_kernel,
        grid=(t // tm, n // tn),
        in_specs=[
            pl.BlockSpec((tm, k), lambda i, j: (i, 0)),
            pl.BlockSpec((k, tn), lambda i, j: (0, j)),
            pl.BlockSpec((tm, tn), lambda i, j: (i, j)),
            pl.BlockSpec((None, 1, tn), lambda i, j: (i // per_b, 0, j)),
        ],
        out_specs=pl.BlockSpec((tm, tn), lambda i, j: (i, j)),
        out_shape=jax.ShapeDtypeStruct((t, n), F32),
        input_output_aliases={2: 0},
        compiler_params=_params("parallel", "arbitrary"),
        name="proj_residual",
    )(y, w, x, gate)


def _ffn_up_kernel(x_ref, g_ref, sh_ref, sc_ref, wg_ref, wv_ref, cg_ref, cv_ref, o_ref, h_scr, tg_scr, tv_scr, *,
                   per_b):
    i = pl.program_id(0)
    j = pl.program_id(1)
    tm = o_ref.shape[0]

    @pl.when(j == 0)
    def _():
        h_scr[...] = _norm_mod(x_ref, g_ref, sh_ref, sc_ref)

    @pl.when(i == 0)
    def _():
        tg_scr[j] = jnp.zeros(tg_scr.shape[1:], F32)
        tv_scr[j] = jnp.zeros(tv_scr.shape[1:], F32)

    h = h_scr[...]
    ug = jnp.dot(h, wg_ref[...], preferred_element_type=F32)
    uv = jnp.dot(h, wv_ref[...], preferred_element_type=F32)

    def conv(u, c_ref, shifted1, shifted2):
        return u * c_ref[2:3, :] + shifted1 * c_ref[1:2, :] + shifted2 * c_ref[0:1, :] + c_ref[3:4, :]

    def glu(yg, yv):
        return (yg * jax.nn.sigmoid(yg) * yv).astype(o_ref.dtype)

    yg = conv(ug, cg_ref, pltpu.roll(ug, 1, 0), pltpu.roll(ug, 2, 0))
    yv = conv(uv, cv_ref, pltpu.roll(uv, 1, 0), pltpu.roll(uv, 2, 0))
    o_ref[...] = glu(yg, yv)

    first = (i % per_b) == 0
    row = lax.broadcasted_iota(jnp.int32, (SUBLANES, ug.shape[1]), 0)

    def head_rows(u, t_scr, c_ref):
        top = u[0:SUBLANES, :]
        tail = jnp.where(first, 0.0, t_scr[j])
        s1 = jnp.where(row == 0, pltpu.roll(tail, 1, 0), pltpu.roll(top, 1, 0))
        s2 = jnp.where(row <= 1, pltpu.roll(tail, 2, 0), pltpu.roll(top, 2, 0))
        t_scr[j] = u[tm - SUBLANES:tm, :]
        return conv(top, c_ref, s1, s2)

    o_ref[0:SUBLANES, :] = glu(head_rows(ug, tg_scr, cg_ref), head_rows(uv, tv_scr, cv_ref))


def _ffn_up(x, g, sh, sc, w_up, conv_tab, seq):
    t, d = x.shape
    f = w_up.shape[1] // 2
    tm = _pick(seq, (1024, 512, 256, 128))
    tn = _pick(f, (512, 256, 128))
    nj = f // tn
    per_b = seq // tm
    return pl.pallas_call(
        functools.partial(_ffn_up_kernel, per_b=per_b),
        grid=(t // tm, nj),
        in_specs=[
            pl.BlockSpec((tm, d), lambda i, j: (i, 0)),
            pl.BlockSpec((1, d), lambda i, j: (0, 0)),
            pl.BlockSpec((None, 1, d), lambda i, j: (i // per_b, 0, 0)),
            pl.BlockSpec((None, 1, d), lambda i, j: (i // per_b, 0, 0)),
            pl.BlockSpec((d, tn), lambda i, j: (0, j)),
            pl.BlockSpec((d, tn), lambda i, j: (0, j + nj)),
            pl.BlockSpec((SUBLANES, tn), lambda i, j: (0, j)),
            pl.BlockSpec((SUBLANES, tn), lambda i, j: (0, j + nj)),
        ],
        out_specs=pl.BlockSpec((tm, tn), lambda i, j: (i, j)),
        out_shape=jax.ShapeDtypeStruct((t, f), BF16),
        scratch_shapes=[
            pltpu.VMEM((tm, d), BF16),
            pltpu.VMEM((nj, SUBLANES, tn), F32),
            pltpu.VMEM((nj, SUBLANES, tn), F32),
        ],
        compiler_params=_params("arbitrary", "arbitrary"),
        name="ffn_up_conv_glu",
    )(x, g, sh, sc, w_up, w_up, conv_tab, conv_tab)


def _swa_kernel(q_ref, kp_ref, kc_ref, vp_ref, vc_ref, sink_ref, o_ref, s_scr, p_scr, e_scr, *, heads):
    n = pl.program_id(1)
    blk = q_ref.shape[0]
    kb = jnp.concatenate([kp_ref[...], kc_ref[...]], axis=0).astype(F32)
    vb = jnp.concatenate([vp_ref[...], vc_ref[...]], axis=0).astype(F32)
    lane = _lane_iota((2 * blk, LANES))
    low = lane < HEAD_DIM
    qi = lax.broadcasted_iota(jnp.int32, (blk, 2 * blk), 0)
    kj = lax.broadcasted_iota(jnp.int32, (blk, 2 * blk), 1)
    rel = qi + blk - kj
    first_valid_key = jnp.where(n > 0, 0, blk)
    mask = (rel >= 0) & (rel < WINDOW) & (kj >= first_valid_key)
    qlow = _lane_iota((blk, LANES)) < HEAD_DIM
    contract_lanes = (((1,), (1,)), ((), ()))

    kv_cache = {}

    def kv_group(g):
        if g not in kv_cache:
            c, half = divmod(g, 2)
            kc = kb[:, c * LANES:(c + 1) * LANES]
            vc = vb[:, c * LANES:(c + 1) * LANES]
            kr = pltpu.roll(kc, HEAD_DIM, 1)
            vr = pltpu.roll(vc, HEAD_DIM, 1)
            k2 = (jnp.where(low, kc, kr) if half == 0 else jnp.where(low, kr, kc)).astype(BF16)
            v_lo, v_hi = (vc, vr) if half == 0 else (vr, vc)
            v_even = jnp.where(low, v_lo, 1.0).astype(BF16)
            v_odd = jnp.where(low, 1.0, v_hi).astype(BF16)
            kv_cache[g] = (k2, v_even, v_odd)
        return kv_cache[g]

    for p in range(heads // 2):
        k2 = kv_group((2 * p) // GQA_GROUP)[0]
        qp = q_ref[:, p * LANES:(p + 1) * LANES]
        for par in range(2):
            qh = jnp.where(qlow if par == 0 else ~qlow, qp, jnp.zeros_like(qp))
            s_scr[2 * p + par] = lax.dot_general(qh, k2, contract_lanes, preferred_element_type=F32)

    s = jnp.where(mask[None], s_scr[...], NEG)
    sink = sink_ref[...]
    m = jnp.maximum(jnp.max(s, axis=-1, keepdims=True), sink)
    p_scr[...] = jnp.exp((s - m).astype(BF16))
    e_scr[...] = jnp.broadcast_to(jnp.exp(sink - m), e_scr.shape)

    for p in range(heads // 2):
        _, v_even, v_odd = kv_group((2 * p) // GQA_GROUP)
        outs = []
        for par in range(2):
            hq = 2 * p + par
            acc = jnp.dot(p_scr[hq], v_even if par == 0 else v_odd, preferred_element_type=F32)
            outs.append(acc / (pltpu.roll(acc, HEAD_DIM, 1) + e_scr[hq]))
        o_ref[:, p * LANES:(p + 1) * LANES] = jnp.where(qlow, outs[0], outs[1]).astype(o_ref.dtype)


def _swa(qkv, sinks, batch, seq):
    t, n = qkv.shape
    heads = sinks.shape[-1]
    nq = heads * HEAD_DIM
    nkv = (n - nq) // 2
    blk = WINDOW
    nb = seq // blk
    kcol = nq // nkv
    return pl.pallas_call(
        functools.partial(_swa_kernel, heads=heads),
        grid=(batch, nb),
        in_specs=[
            pl.BlockSpec((blk, nq), lambda b, i: (b * nb + i, 0)),
            pl.BlockSpec((blk, nkv), lambda b, i: (b * nb + jnp.maximum(i - 1, 0), kcol)),
            pl.BlockSpec((blk, nkv), lambda b, i: (b * nb + i, kcol)),
            pl.BlockSpec((blk, nkv), lambda b, i: (b * nb + jnp.maximum(i - 1, 0), kcol + 1)),
            pl.BlockSpec((blk, nkv), lambda b, i: (b * nb + i, kcol + 1)),
            pl.BlockSpec((heads, 1, 1), lambda b, i: (0, 0, 0)),
        ],
        out_specs=pl.BlockSpec((blk, nq), lambda b, i: (b * nb + i, 0)),
        out_shape=jax.ShapeDtypeStruct((t, nq), BF16),
        scratch_shapes=[
            pltpu.VMEM((heads, blk, 2 * blk), F32),
            pltpu.VMEM((heads, blk, 2 * blk), BF16),
            pltpu.VMEM((heads, blk, LANES), F32),
        ],
        compiler_params=_params("parallel", "parallel"),
        name="swa_attention",
    )(qkv, qkv, qkv, qkv, qkv, sinks.reshape(heads, 1, 1))


def _fox_prep_kernel(gl_ref, f_ref, ft_ref, carry_ref):
    @pl.when(pl.program_id(1) == 0)
    def _():
        carry_ref[...] = jnp.zeros_like(carry_ref)

    ts = gl_ref.shape[0]
    logf = _log_sigmoid(gl_ref[...]) * LOG2E
    f = jnp.dot(_tril_f32(ts), logf, preferred_element_type=F32, precision=lax.Precision.HIGHEST) + carry_ref[...]
    f_ref[...] = f
    ft_ref[...] = f.T
    carry_ref[...] = f[ts - 1:ts, :]


def _fox_prep(gate_logits, batch, seq, ts):
    ns = seq // ts
    return pl.pallas_call(
        _fox_prep_kernel,
        grid=(batch, ns),
        in_specs=[pl.BlockSpec((ts, LANES), lambda b, s: (b * ns + s, 0))],
        out_specs=[
            pl.BlockSpec((ts, LANES), lambda b, s: (b * ns + s, 0)),
            pl.BlockSpec((None, None, LANES, ts), lambda b, s: (b, s, 0, 0)),
        ],
        out_shape=[
            jax.ShapeDtypeStruct((batch * seq, LANES), F32),
            jax.ShapeDtypeStruct((batch, ns, LANES, ts), F32),
        ],
        scratch_shapes=[pltpu.VMEM((1, LANES), F32)],
        compiler_params=_params("parallel", "arbitrary"),
        name="fox_gate_cumsum",
    )(gate_logits)


def _fox_kernel(q_ref, k_ref, v_ref, f_ref, ft_ref, o_ref, m_scr, alpha_scr, acc_scr, s_scr, p_scr, ve_scr, vo_scr,
                *, tq, tk):
    hp = pl.program_id(1)
    i = pl.program_id(2)

    @pl.when(i == 0)
    def _():
        v = v_ref[...]
        vlow = _lane_iota(v.shape) < HEAD_DIM
        ones = jnp.ones_like(v)
        ve_scr[...] = jnp.where(vlow, v, ones)
        vo_scr[...] = jnp.where(vlow, ones, v)

    lane = _lane_iota((tq, LANES))
    low = lane < HEAD_DIM
    q = q_ref[...]
    q_heads = (jnp.where(low, q, jnp.zeros_like(q)), jnp.where(low, jnp.zeros_like(q), q))
    f_tile = f_ref[...]
    contract_lanes = (((1,), (1,)), ((), ()))
    row_in_group = (2 * hp) % SUBLANES

    fq, f0 = [], []
    for par in range(2):
        col = jnp.sum(jnp.where(lane == 2 * hp + par, f_tile, 0.0), axis=-1, keepdims=True)
        f0.append(col[0:1, :])
        fq.append(col - col[0:1, :])

    m_scr[...] = jnp.full(m_scr.shape, NEG, F32)
    acc_scr[...] = jnp.zeros(acc_scr.shape, F32)
    n_full = (i * tq) // tk

    def tile(j, masked):
        start = pl.multiple_of(j * tk, tk)
        k = k_ref[pl.ds(start, tk), :]
        v_heads = (ve_scr[pl.ds(start, tk), :], vo_scr[pl.ds(start, tk), :])
        for par in range(2):
            s_scr[par] = lax.dot_general(q_heads[par], k, contract_lanes, preferred_element_type=F32)
        for par in range(2):
            fk = ft_ref[j, pl.ds(row_in_group + par, 1), :] - f0[par]
            z = s_scr[par] + (fq[par] - fk)
            if masked:
                r = lax.broadcasted_iota(jnp.int32, z.shape, 0) + i * tq
                c = lax.broadcasted_iota(jnp.int32, z.shape, 1) + start
                z = jnp.where(c <= r, z, NEG)
            m_prev = m_scr[par]
            m_new = jnp.maximum(m_prev, jnp.max(z, axis=-1, keepdims=True))
            alpha_scr[par] = jnp.exp2(m_prev - m_new)
            m_scr[par] = m_new
            p_scr[par] = jnp.exp2((z - jnp.concatenate([m_new] * (tk // LANES), axis=1)).astype(BF16))
        for par in range(2):
            acc_scr[par] = alpha_scr[par] * acc_scr[par] + jnp.dot(p_scr[par], v_heads[par],
                                                                   preferred_element_type=F32)

    def body(j, carry):
        tile(j, False)
        return carry

    lax.fori_loop(0, n_full, body, 0)
    tile(n_full, True)

    a0 = acc_scr[0]
    a1 = acc_scr[1]
    o_ref[...] = jnp.where(low, a0 / pltpu.roll(a0, HEAD_DIM, 1), a1 / pltpu.roll(a1, HEAD_DIM, 1)).astype(o_ref.dtype)


def _fox(qkv, f, ft, batch, seq, heads):
    t = qkv.shape[0]
    tk = ft.shape[-1]
    tq = _pick(seq, (512, 256, 128))
    nq = seq // tq
    nk = seq // tk
    pairs = heads // 2
    return pl.pallas_call(
        functools.partial(_fox_kernel, tq=tq, tk=tk),
        grid=(batch, pairs, nq),
        in_specs=[
            pl.BlockSpec((tq, LANES), lambda b, p, i: (b * nq + i, p)),
            pl.BlockSpec((seq, LANES), lambda b, p, i: (b, pairs + p)),
            pl.BlockSpec((seq, LANES), lambda b, p, i: (b, 2 * pairs + p)),
            pl.BlockSpec((tq, LANES), lambda b, p, i: (b * nq + i, 0)),
            pl.BlockSpec((None, nk, SUBLANES, tk), lambda b, p, i: (b, 0, (2 * p) // SUBLANES, 0)),
        ],
        out_specs=pl.BlockSpec((tq, LANES), lambda b, p, i: (b * nq + i, p)),
        out_shape=jax.ShapeDtypeStruct((t, heads * HEAD_DIM), BF16),
        scratch_shapes=[
            pltpu.VMEM((2, tq, LANES), F32),
            pltpu.VMEM((2, tq, LANES), F32),
            pltpu.VMEM((2, tq, LANES), F32),
            pltpu.VMEM((2, tq, tk), F32),
            pltpu.VMEM((2, tq, tk), BF16),
            pltpu.VMEM((seq, LANES), BF16),
            pltpu.VMEM((seq, LANES), BF16),
        ],
        compiler_params=_params("parallel", "parallel", "arbitrary"),
        name="fox_attention",
    )(qkv, qkv, qkv, f, ft)


def _mlstm_kernel(q_ref, k_ref, v_ref, o_ref, g_ref, out_ref, c_scr, m_scr, *, dk, dv):
    @pl.when(pl.program_id(1) == 0)
    def _():
        c_scr[...] = jnp.zeros_like(c_scr)
        m_scr[...] = jnp.zeros_like(m_scr)

    L = g_ref.shape[0]
    gates = g_ref[...]
    b_all = jnp.dot(_tril_f32(L), _log_sigmoid(gates), preferred_element_type=F32,
                    precision=lax.Precision.HIGHEST)
    gates_t = gates.T
    b_all_t = b_all.T
    r = lax.broadcasted_iota(jnp.int32, (L, L), 0)
    c = lax.broadcasted_iota(jnp.int32, (L, L), 1)
    tri = c <= r
    ones_col = (_lane_iota((L, LANES)) == 0).astype(BF16)
    contract_lanes = (((1,), (1,)), ((), ()))
    contract_rows = (((0,), (0,)), ((), ()))

    for hd in range(C_HEADS):
        q = q_ref[:, hd * dk:(hd + 1) * dk]
        k = k_ref[:, hd * dk:(hd + 1) * dk]
        v_aug = jnp.concatenate([v_ref[:, hd * dv:(hd + 1) * dv], ones_col], axis=1)
        b_col = b_all[:, C_HEADS + hd:C_HEADS + hd + 1]
        i_col = gates[:, hd:hd + 1]
        b_row = b_all_t[C_HEADS + hd:C_HEADS + hd + 1, :]
        i_row = gates_t[hd:hd + 1, :]
        m_prev = m_scr[hd:hd + 1, 0:1]
        g_last = b_col[L - 1:L, :]

        dm = jnp.where(tri, b_col - b_row + i_row, NEG)
        inter = b_col + m_prev
        m_t = jnp.maximum(inter, jnp.max(dm, axis=-1, keepdims=True))
        a = jnp.exp(dm - m_t) * lax.dot_general(q, k, contract_lanes, preferred_element_type=F32)
        sc = jnp.exp(inter - m_t)
        state = c_scr[hd]
        num = sc * jnp.dot(q, state.astype(BF16), preferred_element_type=F32) \
            + jnp.dot(a.astype(BF16), v_aug, preferred_element_type=F32)
        den = num[:, dv:dv + 1]
        h_t = num[:, :dv] / jnp.maximum(jnp.abs(den), jnp.exp(-m_t))
        gate_o = jax.nn.sigmoid(o_ref[:, hd * dv:(hd + 1) * dv].astype(F32))
        out_ref[:, hd * dv:(hd + 1) * dv] = (gate_o * h_t).astype(out_ref.dtype)

        wlog = g_last - b_col + i_col
        m_new = jnp.maximum(g_last + m_prev, jnp.max(wlog, axis=0, keepdims=True))
        decay = jnp.exp(g_last + m_prev - m_new)
        kw = (k.astype(F32) * jnp.exp(wlog - m_new)).astype(BF16)
        c_scr[hd] = decay * state + lax.dot_general(kw, v_aug, contract_rows, preferred_element_type=F32)
        m_scr[hd:hd + 1, :] = jnp.broadcast_to(m_new, (1, LANES))


def _mlstm(proj, gates, batch, seq):
    t, n = proj.shape
    dv = n // (3 * C_HEADS)
    dk = dv // 2
    L = _pick(seq, (256, 128, 64))
    nc = seq // L
    hq = C_HEADS * dk
    hv = C_HEADS * dv
    qb, vb = hq // hq, (2 * hq) // hv
    return pl.pallas_call(
        functools.partial(_mlstm_kernel, dk=dk, dv=dv),
        grid=(batch, nc),
        in_specs=[
            pl.BlockSpec((L, hq), lambda b, s: (b * nc + s, 0)),
            pl.BlockSpec((L, hq), lambda b, s: (b * nc + s, qb)),
            pl.BlockSpec((L, hv), lambda b, s: (b * nc + s, vb)),
            pl.BlockSpec((L, hv), lambda b, s: (b * nc + s, vb + 1)),
            pl.BlockSpec((L, LANES), lambda b, s: (b * nc + s, 0)),
        ],
        out_specs=pl.BlockSpec((L, hv), lambda b, s: (b * nc + s, 0)),
        out_shape=jax.ShapeDtypeStruct((t, hv), BF16),
        scratch_shapes=[pltpu.VMEM((C_HEADS, dk, dv + LANES), F32), pltpu.VMEM((SUBLANES, LANES), F32)],
        compiler_params=_params("parallel", "arbitrary"),
        name="mlstm",
    )(proj, proj, proj, proj, gates)


def _final_norm_kernel(x_ref, g_ref, o_ref):
    x = x_ref[...]
    o_ref[...] = x * lax.rsqrt(jnp.mean(x * x, axis=-1, keepdims=True) + EPS) * g_ref[...]


def _final_norm(x, g):
    t, d = x.shape
    tm = _pick(t, (1024, 512, 256, 128))
    return pl.pallas_call(
        _final_norm_kernel,
        grid=(t // tm,),
        in_specs=[pl.BlockSpec((tm, d), lambda i: (i, 0)), pl.BlockSpec((1, d), lambda i: (0, 0))],
        out_specs=pl.BlockSpec((tm, d), lambda i: (i, 0)),
        out_shape=jax.ShapeDtypeStruct((t, d), F32),
        compiler_params=_params("parallel"),
        name="final_norm",
    )(x, g)


def _split_gate_cols(w_in, bias, n_main):
    n_gate = w_in.shape[1] - n_main
    wg = jnp.zeros((w_in.shape[0], LANES), BF16).at[:, :n_gate].set(w_in[:, n_main:].astype(BF16))
    bg = jnp.zeros((1, LANES), F32).at[0, :n_gate].set(bias)
    return w_in[:, :n_main].astype(BF16), wg, bg


def kernel(x, c, positions, mod_w, mod_b, norm_g, final_g, a_w_in, a_sinks, a_w_out, b_w_in, b_f_bias, b_w_out, c_w_in, c_gate_bias, c_w_out, ffn_w_up, ffn_conv_w, ffn_conv_b, ffn_w_down):
    batch, seq, d = x.shape
    depth = mod_w.shape[0]
    t = batch * seq
    xf = x.reshape(t, d)

    mod = _modulation(c, mod_w, mod_b)[:, :batch, :].reshape(depth, batch, 6, 1, d)
    cos, sins = _rope_tables(positions)

    for i in range(depth):
        sh1, sc1, g1, sh2, sc2, g2 = (mod[i, :, r] for r in range(6))
        gain1 = norm_g[i, 0].reshape(1, d)
        gain2 = norm_g[i, 1].reshape(1, d)
        kind, j = i % N_MIXERS, i // N_MIXERS
        if kind == 0:
            qkv = _proj_a(xf, gain1, sh1, sc1, a_w_in[j].astype(BF16), cos, sins, seq)
            y = _swa(qkv, a_sinks[j], batch, seq)
            w_out = a_w_out[j]
        elif kind == 1:
            heads = b_f_bias.shape[-1]
            n_main = 3 * heads * HEAD_DIM
            w, wg, bg = _split_gate_cols(b_w_in[j], b_f_bias[j], n_main)
            qkv, gl = _proj_gate(xf, gain1, sh1, sc1, w, wg, bg, seq, heads * HEAD_DIM, HEAD_DIM ** -0.5 * LOG2E)
            f, ft = _fox_prep(gl, batch, seq, _pick(seq, (512, 256, 128)))
            y = _fox(qkv, f, ft, batch, seq, heads)
            w_out = b_w_out[j]
        else:
            n_main = c_w_in.shape[-1] - 2 * C_HEADS
            dk = n_main // (6 * C_HEADS)
            w, wg, bg = _split_gate_cols(c_w_in[j], c_gate_bias[j], n_main)
            proj, gl = _proj_gate(xf, gain1, sh1, sc1, w, wg, bg, seq, C_HEADS * dk, dk ** -0.5)
            y = _mlstm(proj, gl, batch, seq)
            w_out = c_w_out[j]
        xf = _proj_res(y, w_out.astype(BF16), xf, g1, seq)

        conv_tab = jnp.zeros((SUBLANES, ffn_conv_w.shape[-1]), F32)
        conv_tab = conv_tab.at[:CONV_WIDTH].set(ffn_conv_w[i]).at[CONV_WIDTH].set(ffn_conv_b[i])
        act = _ffn_up(xf, gain2, sh2, sc2, ffn_w_up[i].astype(BF16), conv_tab, seq)
        xf = _proj_res(act, ffn_w_down[i].astype(BF16), xf, g2, seq)

    return _final_norm(xf, final_g.reshape(1, d)).reshape(batch, seq, d)
```

```python
import functools

import jax
import jax.numpy as jnp
from jax import lax
from jax.experimental import pallas as pl
from jax.experimental.pallas import tpu as pltpu

F32 = jnp.float32
BF16 = jnp.bfloat16

EPS = 1e-6
NEG = -1e30
ROPE_THETA = 10000.0
HEAD_DIM = 64
HALF = HEAD_DIM // 2
GQA_GROUP = 8
WINDOW = 128
C_HEADS = 4
N_MIXERS = 3
CONV_WIDTH = 3
LANES = 128
SUBLANES = 8
LOG2E = 1.4426950408889634
VMEM_LIMIT = 56 * 1024 * 1024


def _pick(dim, prefs):
    for p in prefs:
        if dim % p == 0:
            return p
    return dim


def _params(*sem):
    return pltpu.CompilerParams(dimension_semantics=sem, vmem_limit_bytes=VMEM_LIMIT)


def _lane_iota(shape):
    return lax.broadcasted_iota(jnp.int32, shape, len(shape) - 1)


def _log_sigmoid(x):
    return jnp.minimum(x, 0.0) - jnp.log1p(jnp.exp(-jnp.abs(x)))


def _tril_f32(n):
    r = lax.broadcasted_iota(jnp.int32, (n, n), 0)
    c = lax.broadcasted_iota(jnp.int32, (n, n), 1)
    return (c <= r).astype(F32)


def _mod_kernel(c_ref, w_ref, b_ref, o_ref):
    c = c_ref[...]
    act = (c * jax.nn.sigmoid(c)).astype(BF16)
    o_ref[...] = jnp.dot(act, w_ref[...].astype(BF16), preferred_element_type=F32) + b_ref[...]


def _modulation(c, mod_w, mod_b):
    depth, d, n = mod_w.shape
    b = c.shape[0]
    cp = jnp.zeros((SUBLANES, d), F32).at[:b].set(c)
    tn = _pick(n, (1024, 512, 256, 128))
    return pl.pallas_call(
        _mod_kernel,
        grid=(depth, n // tn),
        in_specs=[
            pl.BlockSpec((SUBLANES, d), lambda l, j: (0, 0)),
            pl.BlockSpec((None, d, tn), lambda l, j: (l, 0, j)),
            pl.BlockSpec((None, 1, tn), lambda l, j: (l, 0, j)),
        ],
        out_specs=pl.BlockSpec((None, SUBLANES, tn), lambda l, j: (l, 0, j)),
        out_shape=jax.ShapeDtypeStruct((depth, SUBLANES, n), F32),
        compiler_params=_params("parallel", "parallel"),
        name="modulation",
    )(cp, mod_w, mod_b.reshape(depth, 1, n))


def _rope_table_kernel(pos_ref, freq_ref, cos_ref, sin_ref):
    ang = pos_ref[...] * freq_ref[...]
    lane = _lane_iota(ang.shape)
    cos_ref[...] = jnp.cos(ang)
    s = jnp.sin(ang)
    sin_ref[...] = jnp.where((lane & (HEAD_DIM - 1)) < HALF, -s, s)


def _rope_tables(positions):
    t = positions.size
    pos = positions.reshape(t, 1).astype(F32)
    inv_freq = ROPE_THETA ** (-jnp.arange(0, HEAD_DIM, 2, dtype=F32) / HEAD_DIM)
    freq = jnp.tile(inv_freq, LANES // HALF).reshape(1, LANES)
    ts = _pick(t, (1024, 512, 256, 128))
    return pl.pallas_call(
        _rope_table_kernel,
        grid=(t // ts,),
        in_specs=[pl.BlockSpec((ts, 1), lambda i: (i, 0)), pl.BlockSpec((1, LANES), lambda i: (0, 0))],
        out_specs=[pl.BlockSpec((ts, LANES), lambda i: (i, 0))] * 2,
        out_shape=[jax.ShapeDtypeStruct((t, LANES), F32)] * 2,
        compiler_params=_params("parallel"),
        name="rope_tables",
    )(pos, freq)


def _norm_mod(x_ref, g_ref, sh_ref, sc_ref):
    x = x_ref[...]
    y = x * lax.rsqrt(jnp.mean(x * x, axis=-1, keepdims=True) + EPS)
    y = y * g_ref[...]
    return (y * (1.0 + sc_ref[...]) + sh_ref[...]).astype(BF16)


def _rope(a, cos, sins):
    lane = _lane_iota(a.shape)
    swapped = jnp.where((lane & (HEAD_DIM - 1)) < HALF, pltpu.roll(a, LANES - HALF, 1), pltpu.roll(a, HALF, 1))
    return a * cos + swapped * sins


def _proj_a_kernel(x_ref, g_ref, sh_ref, sc_ref, w_ref, cos_ref, sin_ref, o_ref, *, nq, nkv):
    h = _norm_mod(x_ref, g_ref, sh_ref, sc_ref)
    acc = jnp.dot(h, w_ref[...], preferred_element_type=F32)
    cos = cos_ref[...]
    sins = sin_ref[...]
    for c in range((nq + nkv) // LANES):
        a = _rope(acc[:, c * LANES:(c + 1) * LANES], cos, sins)
        if c * LANES < nq:
            a = a * (HEAD_DIM ** -0.5)
        o_ref[:, c * LANES:(c + 1) * LANES] = a.astype(BF16)
    o_ref[:, nq + nkv:] = acc[:, nq + nkv:].astype(BF16)


def _proj_a(x, g, sh, sc, w, cos, sins, seq):
    t, d = x.shape
    n = w.shape[1]
    nq = d
    nkv = (n - nq) // 2
    tm = _pick(seq, (512, 256, 128))
    per_b = seq // tm
    return pl.pallas_call(
        functools.partial(_proj_a_kernel, nq=nq, nkv=nkv),
        grid=(t // tm,),
        in_specs=[
            pl.BlockSpec((tm, d), lambda i: (i, 0)),
            pl.BlockSpec((1, d), lambda i: (0, 0)),
            pl.BlockSpec((None, 1, d), lambda i: (i // per_b, 0, 0)),
            pl.BlockSpec((None, 1, d), lambda i: (i // per_b, 0, 0)),
            pl.BlockSpec((d, n), lambda i: (0, 0)),
            pl.BlockSpec((tm, LANES), lambda i: (i, 0)),
            pl.BlockSpec((tm, LANES), lambda i: (i, 0)),
        ],
        out_specs=pl.BlockSpec((tm, n), lambda i: (i, 0)),
        out_shape=jax.ShapeDtypeStruct((t, n), BF16),
        compiler_params=_params("parallel"),
        name="proj_swa",
    )(x, g, sh, sc, w, cos, sins)


def _proj_gate_kernel(x_ref, g_ref, sh_ref, sc_ref, w_ref, wg_ref, bg_ref, o_ref, og_ref, h_scr, *,
                      q_tiles, q_scale):
    j = pl.program_id(1)

    @pl.when(j == 0)
    def _():
        h_scr[...] = _norm_mod(x_ref, g_ref, sh_ref, sc_ref)
        og_ref[...] = jnp.dot(h_scr[...], wg_ref[...], preferred_element_type=F32) + bg_ref[...]

    acc = jnp.dot(h_scr[...], w_ref[...], preferred_element_type=F32)
    o_ref[...] = (acc * jnp.where(j < q_tiles, q_scale, 1.0)).astype(BF16)


def _proj_gate(x, g, sh, sc, w, wg, bg, seq, q_cols, q_scale):
    t, d = x.shape
    n = w.shape[1]
    tm = _pick(seq, (1024, 512, 256, 128))
    tn = _pick(q_cols, (1024, 512, 256, 128))
    per_b = seq // tm
    return pl.pallas_call(
        functools.partial(_proj_gate_kernel, q_tiles=q_cols // tn, q_scale=q_scale),
        grid=(t // tm, n // tn),
        in_specs=[
            pl.BlockSpec((tm, d), lambda i, j: (i, 0)),
            pl.BlockSpec((1, d), lambda i, j: (0, 0)),
            pl.BlockSpec((None, 1, d), lambda i, j: (i // per_b, 0, 0)),
            pl.BlockSpec((None, 1, d), lambda i, j: (i // per_b, 0, 0)),
            pl.BlockSpec((d, tn), lambda i, j: (0, j)),
            pl.BlockSpec((d, LANES), lambda i, j: (0, 0)),
            pl.BlockSpec((1, LANES), lambda i, j: (0, 0)),
        ],
        out_specs=[pl.BlockSpec((tm, tn), lambda i, j: (i, j)), pl.BlockSpec((tm, LANES), lambda i, j: (i, 0))],
        out_shape=[jax.ShapeDtypeStruct((t, n), BF16), jax.ShapeDtypeStruct((t, LANES), F32)],
        scratch_shapes=[pltpu.VMEM((tm, d), BF16)],
        compiler_params=_params("parallel", "arbitrary"),
        name="proj_gate",
    )(x, g, sh, sc, w, wg, bg)


def _proj_res_kernel(y_ref, w_ref, x_ref, g_ref, o_ref):
    acc = jnp.dot(y_ref[...], w_ref[...], preferred_element_type=F32)
    o_ref[...] = x_ref[...] + g_ref[...] * acc


def _proj_res(y, w, x, gate, seq):
    t, k = y.shape
    n = w.shape[1]
    tm = _pick(seq, (1024, 512, 256, 128))
    tn = _pick(n, (1024, 512, 256, 128) if k <= n else (512, 256, 128))
    per_b = seq // tm
    return pl.pallas_call(
        _proj_res_kernel,
        grid=(t // tm, n // tn),
        in_specs=[
            pl.BlockSpec((tm, k), lambda i, j: (i, 0)),
            pl.BlockSpec((k, tn), lambda i, j: (0, j)),
            pl.BlockSpec((tm, tn), lambda i, j: (i, j)),
            pl.BlockSpec((None, 1, tn), lambda i, j: (i // per_b, 0, j)),
        ],
        out_specs=pl.BlockSpec((tm, tn), lambda i, j: (i, j)),
        out_shape=jax.ShapeDtypeStruct((t, n), F32),
        input_output_aliases={2: 0},
        compiler_params=_params("parallel", "arbitrary"),
        name="proj_residual",
    )(y, w, x, gate)


def _ffn_up_kernel(x_ref, g_ref, sh_ref, sc_ref, wg_ref, wv_ref, cg_ref, cv_ref, o_ref, h_scr, tg_scr, tv_scr, *,
                   per_b):
    i = pl.program_id(0)
    j = pl.program_id(1)
    tm = o_ref.shape[0]

    @pl.when(j == 0)
    def _():
        h_scr[...] = _norm_mod(x_ref, g_ref, sh_ref, sc_ref)

    @pl.when(i == 0)
    def _():
        tg_scr[j] = jnp.zeros(tg_scr.shape[1:], F32)
        tv_scr[j] = jnp.zeros(tv_scr.shape[1:], F32)

    h = h_scr[...]
    tn = o_ref.shape[1]
    cw = tn
    first = (i % per_b) == 0
    row = lax.broadcasted_iota(jnp.int32, (SUBLANES, cw), 0)

    def conv(u, taps, shifted1, shifted2):
        return u * taps[2:3, :] + shifted1 * taps[1:2, :] + shifted2 * taps[0:1, :] + taps[3:4, :]

    def glu(yg, yv):
        return (yg * jax.nn.sigmoid(yg) * yv).astype(o_ref.dtype)

    for c in range(tn // cw):
        cols = slice(c * cw, (c + 1) * cw)
        ug = jnp.dot(h, wg_ref[:, cols], preferred_element_type=F32)
        uv = jnp.dot(h, wv_ref[:, cols], preferred_element_type=F32)
        taps_g = cg_ref[:, cols]
        taps_v = cv_ref[:, cols]

        yg = conv(ug, taps_g, pltpu.roll(ug, 1, 0), pltpu.roll(ug, 2, 0))
        yv = conv(uv, taps_v, pltpu.roll(uv, 1, 0), pltpu.roll(uv, 2, 0))
        o_ref[:, cols] = glu(yg, yv)

        def head_rows(u, t_scr, taps):
            top = u[0:SUBLANES, :]
            tail = jnp.where(first, 0.0, t_scr[j, :, cols])
            s1 = jnp.where(row == 0, pltpu.roll(tail, 1, 0), pltpu.roll(top, 1, 0))
            s2 = jnp.where(row <= 1, pltpu.roll(tail, 2, 0), pltpu.roll(top, 2, 0))
            t_scr[j, :, cols] = u[tm - SUBLANES:tm, :]
            return conv(top, taps, s1, s2)

        o_ref[0:SUBLANES, cols] = glu(head_rows(ug, tg_scr, taps_g), head_rows(uv, tv_scr, taps_v))


def _ffn_up(x, g, sh, sc, w_up, conv_tab, seq):
    t, d = x.shape
    f = w_up.shape[1] // 2
    tm = _pick(seq, (1024, 512, 256, 128))
    tn = _pick(f, (512, 256, 128))
    nj = f // tn
    per_b = seq // tm
    return pl.pallas_call(
        functools.partial(_ffn_up_kernel, per_b=per_b),
        grid=(t // tm, nj),
        in_specs=[
            pl.BlockSpec((tm, d), lambda i, j: (i, 0)),
            pl.BlockSpec((1, d), lambda i, j: (0, 0)),
            pl.BlockSpec((None, 1, d), lambda i, j: (i // per_b, 0, 0)),
            pl.BlockSpec((None, 1, d), lambda i, j: (i // per_b, 0, 0)),
            pl.BlockSpec((d, tn), lambda i, j: (0, j)),
            pl.BlockSpec((d, tn), lambda i, j: (0, j + nj)),
            pl.BlockSpec((SUBLANES, tn), lambda i, j: (0, j)),
            pl.BlockSpec((SUBLANES, tn), lambda i, j: (0, j + nj)),
        ],
        out_specs=pl.BlockSpec((tm, tn), lambda i, j: (i, j)),
        out_shape=jax.ShapeDtypeStruct((t, f), BF16),
        scratch_shapes=[
            pltpu.VMEM((tm, d), BF16),
            pltpu.VMEM((nj, SUBLANES, tn), F32),
            pltpu.VMEM((nj, SUBLANES, tn), F32),
        ],
        compiler_params=_params("arbitrary", "arbitrary"),
        name="ffn_up_conv_glu",
    )(x, g, sh, sc, w_up, w_up, conv_tab, conv_tab)


def _swa_kernel(q_ref, kp_ref, kc_ref, vp_ref, vc_ref, sink_ref, o_ref, s_scr, p_scr, e_scr, *, heads):
    n = pl.program_id(1)
    blk = q_ref.shape[0]
    kb = jnp.concatenate([kp_ref[...], kc_ref[...]], axis=0).astype(F32)
    vb = jnp.concatenate([vp_ref[...], vc_ref[...]], axis=0).astype(F32)
    lane = _lane_iota((2 * blk, LANES))
    low = lane < HEAD_DIM
    qi = lax.broadcasted_iota(jnp.int32, (blk, 2 * blk), 0)
    kj = lax.broadcasted_iota(jnp.int32, (blk, 2 * blk), 1)
    rel = qi + blk - kj
    first_valid_key = jnp.where(n > 0, 0, blk)
    mask = (rel >= 0) & (rel < WINDOW) & (kj >= first_valid_key)
    qlow = _lane_iota((blk, LANES)) < HEAD_DIM
    contract_lanes = (((1,), (1,)), ((), ()))

    kv_cache = {}

    def kv_group(g):
        if g not in kv_cache:
            c, half = divmod(g, 2)
            kc = kb[:, c * LANES:(c + 1) * LANES]
            vc = vb[:, c * LANES:(c + 1) * LANES]
            kr = pltpu.roll(kc, HEAD_DIM, 1)
            vr = pltpu.roll(vc, HEAD_DIM, 1)
            k2 = (jnp.where(low, kc, kr) if half == 0 else jnp.where(low, kr, kc)).astype(BF16)
            v_lo, v_hi = (vc, vr) if half == 0 else (vr, vc)
            v_even = jnp.where(low, v_lo, 1.0).astype(BF16)
            v_odd = jnp.where(low, 1.0, v_hi).astype(BF16)
            kv_cache[g] = (k2, v_even, v_odd)
        return kv_cache[g]

    for p in range(heads // 2):
        k2 = kv_group((2 * p) // GQA_GROUP)[0]
        qp = q_ref[:, p * LANES:(p + 1) * LANES]
        for par in range(2):
            qh = jnp.where(qlow if par == 0 else ~qlow, qp, jnp.zeros_like(qp))
            s_scr[2 * p + par] = lax.dot_general(qh, k2, contract_lanes, preferred_element_type=F32)

    s = jnp.where(mask[None], s_scr[...], NEG)
    sink = sink_ref[...]
    m = jnp.broadcast_to(jnp.maximum(jnp.max(s, axis=-1, keepdims=True), sink), e_scr.shape)
    p_scr[...] = jnp.exp((s - jnp.concatenate([m] * (2 * blk // LANES), axis=-1)).astype(BF16))
    e_scr[...] = jnp.exp(sink - m)

    for p in range(heads // 2):
        _, v_even, v_odd = kv_group((2 * p) // GQA_GROUP)
        outs = []
        for par in range(2):
            hq = 2 * p + par
            acc = jnp.dot(p_scr[hq], v_even if par == 0 else v_odd, preferred_element_type=F32)
            outs.append(acc / (pltpu.roll(acc, HEAD_DIM, 1) + e_scr[hq]))
        o_ref[:, p * LANES:(p + 1) * LANES] = jnp.where(qlow, outs[0], outs[1]).astype(o_ref.dtype)


def _swa(qkv, sinks, batch, seq):
    t, n = qkv.shape
    heads = sinks.shape[-1]
    nq = heads * HEAD_DIM
    nkv = (n - nq) // 2
    blk = WINDOW
    nb = seq // blk
    kcol = nq // nkv
    return pl.pallas_call(
        functools.partial(_swa_kernel, heads=heads),
        grid=(batch, nb),
        in_specs=[
            pl.BlockSpec((blk, nq), lambda b, i: (b * nb + i, 0)),
            pl.BlockSpec((blk, nkv), lambda b, i: (b * nb + jnp.maximum(i - 1, 0), kcol)),
            pl.BlockSpec((blk, nkv), lambda b, i: (b * nb + i, kcol)),
            pl.BlockSpec((blk, nkv), lambda b, i: (b * nb + jnp.maximum(i - 1, 0), kcol + 1)),
            pl.BlockSpec((blk, nkv), lambda b, i: (b * nb + i, kcol + 1)),
            pl.BlockSpec((heads, 1, 1), lambda b, i: (0, 0, 0)),
        ],
        out_specs=pl.BlockSpec((blk, nq), lambda b, i: (b * nb + i, 0)),
        out_shape=jax.ShapeDtypeStruct((t, nq), BF16),
        scratch_shapes=[
            pltpu.VMEM((heads, blk, 2 * blk), F32),
            pltpu.VMEM((heads, blk, 2 * blk), BF16),
            pltpu.VMEM((heads, blk, LANES), F32),
        ],
        compiler_params=_params("parallel", "parallel"),
        name="swa_attention",
    )(qkv, qkv, qkv, qkv, qkv, sinks.reshape(heads, 1, 1))


def _fox_prep_kernel(gl_ref, f_ref, ft_ref, carry_ref):
    @pl.when(pl.program_id(1) == 0)
    def _():
        carry_ref[...] = jnp.zeros_like(carry_ref)

    ts = gl_ref.shape[0]
    logf = _log_sigmoid(gl_ref[...]) * LOG2E
    f = jnp.dot(_tril_f32(ts), logf, preferred_element_type=F32, precision=lax.Precision.HIGHEST) + carry_ref[...]
    f_ref[...] = f
    ft_ref[...] = f.T
    carry_ref[...] = f[ts - 1:ts, :]


def _fox_prep(gate_logits, batch, seq, ts):
    ns = seq // ts
    return pl.pallas_call(
        _fox_prep_kernel,
        grid=(batch, ns),
        in_specs=[pl.BlockSpec((ts, LANES), lambda b, s: (b * ns + s, 0))],
        out_specs=[
            pl.BlockSpec((ts, LANES), lambda b, s: (b * ns + s, 0)),
            pl.BlockSpec((None, None, LANES, ts), lambda b, s: (b, s, 0, 0)),
        ],
        out_shape=[
            jax.ShapeDtypeStruct((batch * seq, LANES), F32),
            jax.ShapeDtypeStruct((batch, ns, LANES, ts), F32),
        ],
        scratch_shapes=[pltpu.VMEM((1, LANES), F32)],
        compiler_params=_params("parallel", "arbitrary"),
        name="fox_gate_cumsum",
    )(gate_logits)


def _fox_kernel(q_ref, k_ref, v_ref, f_ref, ft_ref, o_ref, m_scr, alpha_scr, acc_scr, s_scr, p_scr, ve_scr, vo_scr,
                *, tq, tk):
    hp = pl.program_id(1)
    i = pl.program_id(2)

    @pl.when(i == 0)
    def _():
        v = v_ref[...]
        vlow = _lane_iota(v.shape) < HEAD_DIM
        ones = jnp.ones_like(v)
        ve_scr[...] = jnp.where(vlow, v, ones)
        vo_scr[...] = jnp.where(vlow, ones, v)

    lane = _lane_iota((tq, LANES))
    low = lane < HEAD_DIM
    q = q_ref[...]
    q_heads = (jnp.where(low, q, jnp.zeros_like(q)), jnp.where(low, jnp.zeros_like(q), q))
    f_tile = f_ref[...]
    contract_lanes = (((1,), (1,)), ((), ()))
    row_in_group = (2 * hp) % SUBLANES

    fq, f0 = [], []
    for par in range(2):
        col = jnp.sum(jnp.where(lane == 2 * hp + par, f_tile, 0.0), axis=-1, keepdims=True)
        f0.append(col[0:1, :])
        fq.append(col - col[0:1, :])

    m_scr[...] = jnp.full(m_scr.shape, NEG, F32)
    acc_scr[...] = jnp.zeros(acc_scr.shape, F32)
    n_full = (i * tq) // tk

    def tile(j, masked):
        start = pl.multiple_of(j * tk, tk)
        k = k_ref[pl.ds(start, tk), :]
        v_heads = (ve_scr[pl.ds(start, tk), :], vo_scr[pl.ds(start, tk), :])
        for par in range(2):
            s_scr[par] = lax.dot_general(q_heads[par], k, contract_lanes, preferred_element_type=F32)
        for par in range(2):
            fk = ft_ref[j, pl.ds(row_in_group + par, 1), :] - f0[par]
            z = s_scr[par] + (fq[par] - fk)
            if masked:
                r = lax.broadcasted_iota(jnp.int32, z.shape, 0) + i * tq
                c = lax.broadcasted_iota(jnp.int32, z.shape, 1) + start
                z = jnp.where(c <= r, z, NEG)
            m_prev = m_scr[par]
            m_new = jnp.maximum(m_prev, jnp.max(z, axis=-1, keepdims=True))
            alpha_scr[par] = jnp.exp2(m_prev - m_new)
            m_scr[par] = m_new
            p_scr[par] = jnp.exp2((z - jnp.concatenate([m_new] * (tk // LANES), axis=1)).astype(BF16))
        for par in range(2):
            acc_scr[par] = alpha_scr[par] * acc_scr[par] + jnp.dot(p_scr[par], v_heads[par],
                                                                   preferred_element_type=F32)

    def body(j, carry):
        tile(j, False)
        return carry

    lax.fori_loop(0, n_full, body, 0)
    tile(n_full, True)

    a0 = acc_scr[0]
    a1 = acc_scr[1]
    o_ref[...] = jnp.where(low, a0 / pltpu.roll(a0, HEAD_DIM, 1), a1 / pltpu.roll(a1, HEAD_DIM, 1)).astype(o_ref.dtype)


def _fox(qkv, f, ft, batch, seq, heads):
    t = qkv.shape[0]
    tk = ft.shape[-1]
    tq = _pick(seq, (512, 256, 128))
    nq = seq // tq
    nk = seq // tk
    pairs = heads // 2
    return pl.pallas_call(
        functools.partial(_fox_kernel, tq=tq, tk=tk),
        grid=(batch, pairs, nq),
        in_specs=[
            pl.BlockSpec((tq, LANES), lambda b, p, i: (b * nq + i, p)),
            pl.BlockSpec((seq, LANES), lambda b, p, i: (b, pairs + p)),
            pl.BlockSpec((seq, LANES), lambda b, p, i: (b, 2 * pairs + p)),
            pl.BlockSpec((tq, LANES), lambda b, p, i: (b * nq + i, 0)),
            pl.BlockSpec((None, nk, SUBLANES, tk), lambda b, p, i: (b, 0, (2 * p) // SUBLANES, 0)),
        ],
        out_specs=pl.BlockSpec((tq, LANES), lambda b, p, i: (b * nq + i, p)),
        out_shape=jax.ShapeDtypeStruct((t, heads * HEAD_DIM), BF16),
        scratch_shapes=[
            pltpu.VMEM((2, tq, LANES), F32),
            pltpu.VMEM((2, tq, LANES), F32),
            pltpu.VMEM((2, tq, LANES), F32),
            pltpu.VMEM((2, tq, tk), F32),
            pltpu.VMEM((2, tq, tk), BF16),
            pltpu.VMEM((seq, LANES), BF16),
            pltpu.VMEM((seq, LANES), BF16),
        ],
        compiler_params=_params("parallel", "parallel", "arbitrary"),
        name="fox_attention",
    )(qkv, qkv, qkv, f, ft)


def _mlstm_kernel(q_ref, k_ref, v_ref, o_ref, g_ref, out_ref, c_scr, m_scr, *, dk, dv):
    @pl.when(pl.program_id(1) == 0)
    def _():
        c_scr[...] = jnp.zeros_like(c_scr)
        m_scr[...] = jnp.zeros_like(m_scr)

    L = g_ref.shape[0]
    gates = g_ref[...]
    b_all = jnp.dot(_tril_f32(L), _log_sigmoid(gates), preferred_element_type=F32,
                    precision=lax.Precision.HIGHEST)
    gates_t = gates.T
    b_all_t = b_all.T
    r = lax.broadcasted_iota(jnp.int32, (L, L), 0)
    c = lax.broadcasted_iota(jnp.int32, (L, L), 1)
    tri = c <= r
    ones_col = (_lane_iota((L, LANES)) == 0).astype(BF16)
    contract_lanes = (((1,), (1,)), ((), ()))
    contract_rows = (((0,), (0,)), ((), ()))

    for hd in range(C_HEADS):
        q = q_ref[:, hd * dk:(hd + 1) * dk]
        k = k_ref[:, hd * dk:(hd + 1) * dk]
        v_aug = jnp.concatenate([v_ref[:, hd * dv:(hd + 1) * dv], ones_col], axis=1)
        b_col = b_all[:, C_HEADS + hd:C_HEADS + hd + 1]
        i_col = gates[:, hd:hd + 1]
        b_row = b_all_t[C_HEADS + hd:C_HEADS + hd + 1, :]
        i_row = gates_t[hd:hd + 1, :]
        m_prev = m_scr[hd:hd + 1, 0:1]
        g_last = b_col[L - 1:L, :]

        dm = jnp.where(tri, b_col - b_row + i_row, NEG)
        inter = b_col + m_prev
        m_t = jnp.maximum(inter, jnp.max(dm, axis=-1, keepdims=True))
        a = jnp.exp(dm - m_t) * lax.dot_general(q, k, contract_lanes, preferred_element_type=F32)
        sc = jnp.exp(inter - m_t)
        state = c_scr[hd]
        num = sc * jnp.dot(q, state.astype(BF16), preferred_element_type=F32) \
            + jnp.dot(a.astype(BF16), v_aug, preferred_element_type=F32)
        den = num[:, dv:dv + 1]
        h_t = num[:, :dv] / jnp.maximum(jnp.abs(den), jnp.exp(-m_t))
        gate_o = jax.nn.sigmoid(o_ref[:, hd * dv:(hd + 1) * dv].astype(F32))
        out_ref[:, hd * dv:(hd + 1) * dv] = (gate_o * h_t).astype(out_ref.dtype)

        wlog = g_last - b_col + i_col
        m_new = jnp.maximum(g_last + m_prev, jnp.max(wlog, axis=0, keepdims=True))
        decay = jnp.exp(g_last + m_prev - m_new)
        kw = (k.astype(F32) * jnp.exp(wlog - m_new)).astype(BF16)
        c_scr[hd] = decay * state + lax.dot_general(kw, v_aug, contract_rows, preferred_element_type=F32)
        m_scr[hd:hd + 1, :] = jnp.broadcast_to(m_new, (1, LANES))


def _mlstm(proj, gates, batch, seq):
    t, n = proj.shape
    dv = n // (3 * C_HEADS)
    dk = dv // 2
    L = _pick(seq, (256, 128, 64))
    nc = seq // L
    hq = C_HEADS * dk
    hv = C_HEADS * dv
    qb, vb = hq // hq, (2 * hq) // hv
    return pl.pallas_call(
        functools.partial(_mlstm_kernel, dk=dk, dv=dv),
        grid=(batch, nc),
        in_specs=[
            pl.BlockSpec((L, hq), lambda b, s: (b * nc + s, 0)),
            pl.BlockSpec((L, hq), lambda b, s: (b * nc + s, qb)),
            pl.BlockSpec((L, hv), lambda b, s: (b * nc + s, vb)),
            pl.BlockSpec((L, hv), lambda b, s: (b * nc + s, vb + 1)),
            pl.BlockSpec((L, LANES), lambda b, s: (b * nc + s, 0)),
        ],
        out_specs=pl.BlockSpec((L, hv), lambda b, s: (b * nc + s, 0)),
        out_shape=jax.ShapeDtypeStruct((t, hv), BF16),
        scratch_shapes=[pltpu.VMEM((C_HEADS, dk, dv + LANES), F32), pltpu.VMEM((SUBLANES, LANES), F32)],
        compiler_params=_params("parallel", "arbitrary"),
        name="mlstm",
    )(proj, proj, proj, proj, gates)


def _final_norm_kernel(x_ref, g_ref, o_ref):
    x = x_ref[...]
    o_ref[...] = x * lax.rsqrt(jnp.mean(x * x, axis=-1, keepdims=True) + EPS) * g_ref[...]


def _final_norm(x, g):
    t, d = x.shape
    tm = _pick(t, (1024, 512, 256, 128))
    return pl.pallas_call(
        _final_norm_kernel,
        grid=(t // tm,),
        in_specs=[pl.BlockSpec((tm, d), lambda i: (i, 0)), pl.BlockSpec((1, d), lambda i: (0, 0))],
        out_specs=pl.BlockSpec((tm, d), lambda i: (i, 0)),
        out_shape=jax.ShapeDtypeStruct((t, d), F32),
        compiler_params=_params("parallel"),
        name="final_norm",
    )(x, g)


def _split_gate_cols(w_in, bias, n_main):
    n_gate = w_in.shape[1] - n_main
    wg = jnp.zeros((w_in.shape[0], LANES), BF16).at[:, :n_gate].set(w_in[:, n_main:].astype(BF16))
    bg = jnp.zeros((1, LANES), F32).at[0, :n_gate].set(bias)
    return w_in[:, :n_main].astype(BF16), wg, bg


def kernel(x, c, positions, mod_w, mod_b, norm_g, final_g, a_w_in, a_sinks, a_w_out, b_w_in, b_f_bias, b_w_out, c_w_in, c_gate_bias, c_w_out, ffn_w_up, ffn_conv_w, ffn_conv_b, ffn_w_down):
    batch, seq, d = x.shape
    depth = mod_w.shape[0]
    t = batch * seq
    xf = x.reshape(t, d)

    mod = _modulation(c, mod_w, mod_b)[:, :batch, :].reshape(depth, batch, 6, 1, d)
    cos, sins = _rope_tables(positions)

    for i in range(depth):
        sh1, sc1, g1, sh2, sc2, g2 = (mod[i, :, r] for r in range(6))
        gain1 = norm_g[i, 0].reshape(1, d)
        gain2 = norm_g[i, 1].reshape(1, d)
        kind, j = i % N_MIXERS, i // N_MIXERS
        if kind == 0:
            qkv = _proj_a(xf, gain1, sh1, sc1, a_w_in[j].astype(BF16), cos, sins, seq)
            y = _swa(qkv, a_sinks[j], batch, seq)
            w_out = a_w_out[j]
        elif kind == 1:
            heads = b_f_bias.shape[-1]
            n_main = 3 * heads * HEAD_DIM
            w, wg, bg = _split_gate_cols(b_w_in[j], b_f_bias[j], n_main)
            qkv, gl = _proj_gate(xf, gain1, sh1, sc1, w, wg, bg, seq, heads * HEAD_DIM, HEAD_DIM ** -0.5 * LOG2E)
            f, ft = _fox_prep(gl, batch, seq, _pick(seq, (512, 256, 128)))
            y = _fox(qkv, f, ft, batch, seq, heads)
            w_out = b_w_out[j]
        else:
            n_main = c_w_in.shape[-1] - 2 * C_HEADS
            dk = n_main // (6 * C_HEADS)
            w, wg, bg = _split_gate_cols(c_w_in[j], c_gate_bias[j], n_main)
            proj, gl = _proj_gate(xf, gain1, sh1, sc1, w, wg, bg, seq, C_HEADS * dk, dk ** -0.5)
            y = _mlstm(proj, gl, batch, seq)
            w_out = c_w_out[j]
        xf = _proj_res(y, w_out.astype(BF16), xf, g1, seq)

        conv_tab = jnp.zeros((SUBLANES, ffn_conv_w.shape[-1]), F32)
        conv_tab = conv_tab.at[:CONV_WIDTH].set(ffn_conv_w[i]).at[CONV_WIDTH].set(ffn_conv_b[i])
        act = _ffn_up(xf, gain2, sh2, sc2, ffn_w_up[i].astype(BF16), conv_tab, seq)
        xf = _proj_res(act, ffn_w_down[i].astype(BF16), xf, g2, seq)

    return _final_norm(xf, final_g.reshape(1, d)).reshape(batch, seq, d)
```

```python
import functools

import jax
import jax.numpy as jnp
from jax import lax
from jax.experimental import pallas as pl
from jax.experimental.pallas import tpu as pltpu

F32 = jnp.float32
BF16 = jnp.bfloat16

EPS = 1e-6
NEG = -1e30
ROPE_THETA = 10000.0
HEAD_DIM = 64
HALF = HEAD_DIM // 2
GQA_GROUP = 8
WINDOW = 128
C_HEADS = 4
N_MIXERS = 3
CONV_WIDTH = 3
LANES = 128
SUBLANES = 8
LOG2E = 1.4426950408889634
VMEM_LIMIT = 56 * 1024 * 1024


def _pick(dim, prefs):
    for p in prefs:
        if dim % p == 0:
            return p
    return dim


def _params(*sem):
    return pltpu.CompilerParams(dimension_semantics=sem, vmem_limit_bytes=VMEM_LIMIT)


def _lane_iota(shape):
    return lax.broadcasted_iota(jnp.int32, shape, len(shape) - 1)


def _log_sigmoid(x):
    return jnp.minimum(x, 0.0) - jnp.log1p(jnp.exp(-jnp.abs(x)))


def _tril_f32(n):
    r = lax.broadcasted_iota(jnp.int32, (n, n), 0)
    c = lax.broadcasted_iota(jnp.int32, (n, n), 1)
    return (c <= r).astype(F32)


def _mod_kernel(c_ref, w_ref, b_ref, o_ref):
    c = c_ref[...]
    act = (c * jax.nn.sigmoid(c)).astype(BF16)
    o_ref[...] = jnp.dot(act, w_ref[...].astype(BF16), preferred_element_type=F32) + b_ref[...]


def _modulation(c, mod_w, mod_b):
    depth, d, n = mod_w.shape
    b = c.shape[0]
    cp = jnp.zeros((SUBLANES, d), F32).at[:b].set(c)
    tn = _pick(n, (1024, 512, 256, 128))
    return pl.pallas_call(
        _mod_kernel,
        grid=(depth, n // tn),
        in_specs=[
            pl.BlockSpec((SUBLANES, d), lambda l, j: (0, 0)),
            pl.BlockSpec((None, d, tn), lambda l, j: (l, 0, j)),
            pl.BlockSpec((None, 1, tn), lambda l, j: (l, 0, j)),
        ],
        out_specs=pl.BlockSpec((None, SUBLANES, tn), lambda l, j: (l, 0, j)),
        out_shape=jax.ShapeDtypeStruct((depth, SUBLANES, n), F32),
        compiler_params=_params("parallel", "parallel"),
        name="modulation",
    )(cp, mod_w, mod_b.reshape(depth, 1, n))


def _rope_table_kernel(pos_ref, freq_ref, cos_ref, sin_ref):
    ang = pos_ref[...] * freq_ref[...]
    lane = _lane_iota(ang.shape)
    cos_ref[...] = jnp.cos(ang)
    s = jnp.sin(ang)
    sin_ref[...] = jnp.where((lane & (HEAD_DIM - 1)) < HALF, -s, s)


def _rope_tables(positions):
    t = positions.size
    pos = positions.reshape(t, 1).astype(F32)
    inv_freq = ROPE_THETA ** (-jnp.arange(0, HEAD_DIM, 2, dtype=F32) / HEAD_DIM)
    freq = jnp.tile(inv_freq, LANES // HALF).reshape(1, LANES)
    ts = _pick(t, (1024, 512, 256, 128))
    return pl.pallas_call(
        _rope_table_kernel,
        grid=(t // ts,),
        in_specs=[pl.BlockSpec((ts, 1), lambda i: (i, 0)), pl.BlockSpec((1, LANES), lambda i: (0, 0))],
        out_specs=[pl.BlockSpec((ts, LANES), lambda i: (i, 0))] * 2,
        out_shape=[jax.ShapeDtypeStruct((t, LANES), F32)] * 2,
        compiler_params=_params("parallel"),
        name="rope_tables",
    )(pos, freq)


def _norm_mod(x_ref, g_ref, sh_ref, sc_ref):
    x = x_ref[...]
    y = x * lax.rsqrt(jnp.mean(x * x, axis=-1, keepdims=True) + EPS)
    y = y * g_ref[...]
    return (y * (1.0 + sc_ref[...]) + sh_ref[...]).astype(BF16)


def _rope(a, cos, sins):
    lane = _lane_iota(a.shape)
    swapped = jnp.where((lane & (HEAD_DIM - 1)) < HALF, pltpu.roll(a, LANES - HALF, 1), pltpu.roll(a, HALF, 1))
    return a * cos + swapped * sins


def _proj_a_kernel(x_ref, g_ref, sh_ref, sc_ref, w_ref, cos_ref, sin_ref, o_ref, *, nq, nkv):
    h = _norm_mod(x_ref, g_ref, sh_ref, sc_ref)
    acc = jnp.dot(h, w_ref[...], preferred_element_type=F32)
    cos = cos_ref[...]
    sins = sin_ref[...]
    for c in range((nq + nkv) // LANES):
        a = _rope(acc[:, c * LANES:(c + 1) * LANES], cos, sins)
        if c * LANES < nq:
            a = a * (HEAD_DIM ** -0.5)
        o_ref[:, c * LANES:(c + 1) * LANES] = a.astype(BF16)
    o_ref[:, nq + nkv:] = acc[:, nq + nkv:].astype(BF16)


def _proj_a(x, g, sh, sc, w, cos, sins, seq):
    t, d = x.shape
    n = w.shape[1]
    nq = d
    nkv = (n - nq) // 2
    tm = _pick(seq, (512, 256, 128))
    per_b = seq // tm
    return pl.pallas_call(
        functools.partial(_proj_a_kernel, nq=nq, nkv=nkv),
        grid=(t // tm,),
        in_specs=[
            pl.BlockSpec((tm, d), lambda i: (i, 0)),
            pl.BlockSpec((1, d), lambda i: (0, 0)),
            pl.BlockSpec((None, 1, d), lambda i: (i // per_b, 0, 0)),
            pl.BlockSpec((None, 1, d), lambda i: (i // per_b, 0, 0)),
            pl.BlockSpec((d, n), lambda i: (0, 0)),
            pl.BlockSpec((tm, LANES), lambda i: (i, 0)),
            pl.BlockSpec((tm, LANES), lambda i: (i, 0)),
        ],
        out_specs=pl.BlockSpec((tm, n), lambda i: (i, 0)),
        out_shape=jax.ShapeDtypeStruct((t, n), BF16),
        compiler_params=_params("parallel"),
        name="proj_swa",
    )(x, g, sh, sc, w, cos, sins)


def _proj_gate_kernel(x_ref, g_ref, sh_ref, sc_ref, w_ref, wg_ref, bg_ref, o_ref, og_ref, h_scr, *,
                      q_tiles, q_scale):
    j = pl.program_id(1)

    @pl.when(j == 0)
    def _():
        h_scr[...] = _norm_mod(x_ref, g_ref, sh_ref, sc_ref)
        og_ref[...] = jnp.dot(h_scr[...], wg_ref[...], preferred_element_type=F32) + bg_ref[...]

    acc = jnp.dot(h_scr[...], w_ref[...], preferred_element_type=F32)
    o_ref[...] = (acc * jnp.where(j < q_tiles, q_scale, 1.0)).astype(BF16)


def _proj_gate(x, g, sh, sc, w, wg, bg, seq, q_cols, q_scale):
    t, d = x.shape
    n = w.shape[1]
    tm = _pick(seq, (1024, 512, 256, 128))
    tn = _pick(q_cols, (1024, 512, 256, 128))
    per_b = seq // tm
    return pl.pallas_call(
        functools.partial(_proj_gate_kernel, q_tiles=q_cols // tn, q_scale=q_scale),
        grid=(t // tm, n // tn),
        in_specs=[
            pl.BlockSpec((tm, d), lambda i, j: (i, 0)),
            pl.BlockSpec((1, d), lambda i, j: (0, 0)),
            pl.BlockSpec((None, 1, d), lambda i, j: (i // per_b, 0, 0)),
            pl.BlockSpec((None, 1, d), lambda i, j: (i // per_b, 0, 0)),
            pl.BlockSpec((d, tn), lambda i, j: (0, j)),
            pl.BlockSpec((d, LANES), lambda i, j: (0, 0)),
            pl.BlockSpec((1, LANES), lambda i, j: (0, 0)),
        ],
        out_specs=[pl.BlockSpec((tm, tn), lambda i, j: (i, j)), pl.BlockSpec((tm, LANES), lambda i, j: (i, 0))],
        out_shape=[jax.ShapeDtypeStruct((t, n), BF16), jax.ShapeDtypeStruct((t, LANES), F32)],
        scratch_shapes=[pltpu.VMEM((tm, d), BF16)],
        compiler_params=_params("parallel", "arbitrary"),
        name="proj_gate",
    )(x, g, sh, sc, w, wg, bg)


def _proj_res_kernel(y_ref, w_ref, x_ref, g_ref, o_ref):
    acc = jnp.dot(y_ref[...], w_ref[...], preferred_element_type=F32)
    o_ref[...] = x_ref[...] + g_ref[...] * acc


def _proj_res(y, w, x, gate, seq):
    t, k = y.shape
    n = w.shape[1]
    tm = _pick(seq, (1024, 512, 256, 128))
    tn = _pick(n, (1024, 512, 256, 128) if k <= n else (512, 256, 128))
    per_b = seq // tm
    return pl.pallas_call(
        _proj_res_kernel,
        grid=(t // tm, n // tn),
        in_specs=[
            pl.BlockSpec((tm, k), lambda i, j: (i, 0)),
            pl.BlockSpec((k, tn), lambda i, j: (0, j)),
            pl.BlockSpec((tm, tn), lambda i, j: (i, j)),
            pl.BlockSpec((None, 1, tn), lambda i, j: (i // per_b, 0, j)),
        ],
        out_specs=pl.BlockSpec((tm, tn), lambda i, j: (i, j)),
        out_shape=jax.ShapeDtypeStruct((t, n), F32),
        input_output_aliases={2: 0},
        compiler_params=_params("parallel", "arbitrary"),
        name="proj_residual",
    )(y, w, x, gate)


def _ffn_up_kernel(x_ref, g_ref, sh_ref, sc_ref, wg_ref, wv_ref, cg_ref, cv_ref, o_ref, h_scr, tg_scr, tv_scr, *,
                   per_b):
    i = pl.program_id(0)
    j = pl.program_id(1)
    tm = o_ref.shape[0]

    @pl.when(j == 0)
    def _():
        h_scr[...] = _norm_mod(x_ref, g_ref, sh_ref, sc_ref)

    @pl.when(i == 0)
    def _():
        tg_scr[j] = jnp.zeros(tg_scr.shape[1:], F32)
        tv_scr[j] = jnp.zeros(tv_scr.shape[1:], F32)

    h = h_scr[...]
    ug = jnp.dot(h, wg_ref[...], preferred_element_type=F32)
    uv = jnp.dot(h, wv_ref[...], preferred_element_type=F32)

    def conv(u, c_ref, shifted1, shifted2):
        return u * c_ref[2:3, :] + shifted1 * c_ref[1:2, :] + shifted2 * c_ref[0:1, :] + c_ref[3:4, :]

    def glu(yg, yv):
        return (yg * jax.nn.sigmoid(yg) * yv).astype(o_ref.dtype)

    yg = conv(ug, cg_ref, pltpu.roll(ug, 1, 0), pltpu.roll(ug, 2, 0))
    yv = conv(uv, cv_ref, pltpu.roll(uv, 1, 0), pltpu.roll(uv, 2, 0))
    o_ref[...] = glu(yg, yv)

    first = (i % per_b) == 0
    row = lax.broadcasted_iota(jnp.int32, (SUBLANES, ug.shape[1]), 0)

    def head_rows(u, t_scr, c_ref):
        top = u[0:SUBLANES, :]
        tail = jnp.where(first, 0.0, t_scr[j])
        s1 = jnp.where(row == 0, pltpu.roll(tail, 1, 0), pltpu.roll(top, 1, 0))
        s2 = jnp.where(row <= 1, pltpu.roll(tail, 2, 0), pltpu.roll(top, 2, 0))
        t_scr[j] = u[tm - SUBLANES:tm, :]
        return conv(top, c_ref, s1, s2)

    o_ref[0:SUBLANES, :] = glu(head_rows(ug, tg_scr, cg_ref), head_rows(uv, tv_scr, cv_ref))


def _ffn_up(x, g, sh, sc, w_up, conv_tab, seq):
    t, d = x.shape
    f = w_up.shape[1] // 2
    tm = _pick(seq, (1024, 512, 256, 128))
    tn = _pick(f, (512, 256, 128))
    nj = f // tn
    per_b = seq // tm
    return pl.pallas_call(
        functools.partial(_ffn_up_kernel, per_b=per_b),
        grid=(t // tm, nj),
        in_specs=[
            pl.BlockSpec((tm, d), lambda i, j: (i, 0)),
            pl.BlockSpec((1, d), lambda i, j: (0, 0)),
            pl.BlockSpec((None, 1, d), lambda i, j: (i // per_b, 0, 0)),
            pl.BlockSpec((None, 1, d), lambda i, j: (i // per_b, 0, 0)),
            pl.BlockSpec((d, tn), lambda i, j: (0, j)),
            pl.BlockSpec((d, tn), lambda i, j: (0, j + nj)),
            pl.BlockSpec((SUBLANES, tn), lambda i, j: (0, j)),
            pl.BlockSpec((SUBLANES, tn), lambda i, j: (0, j + nj)),
        ],
        out_specs=pl.BlockSpec((tm, tn), lambda i, j: (i, j)),
        out_shape=jax.ShapeDtypeStruct((t, f), BF16),
        scratch_shapes=[
            pltpu.VMEM((tm, d), BF16),
            pltpu.VMEM((nj, SUBLANES, tn), F32),
            pltpu.VMEM((nj, SUBLANES, tn), F32),
        ],
        compiler_params=_params("arbitrary", "arbitrary"),
        name="ffn_up_conv_glu",
    )(x, g, sh, sc, w_up, w_up, conv_tab, conv_tab)


def _swa_kernel(q_ref, kp_ref, kc_ref, vp_ref, vc_ref, sink_ref, o_ref, s_scr, p_scr, e_scr, *, heads):
    n = pl.program_id(1)
    blk = q_ref.shape[0]
    kb = jnp.concatenate([kp_ref[...], kc_ref[...]], axis=0).astype(F32)
    vb = jnp.concatenate([vp_ref[...], vc_ref[...]], axis=0).astype(F32)
    lane = _lane_iota((2 * blk, LANES))
    low = lane < HEAD_DIM
    qi = lax.broadcasted_iota(jnp.int32, (blk, 2 * blk), 0)
    kj = lax.broadcasted_iota(jnp.int32, (blk, 2 * blk), 1)
    rel = qi + blk - kj
    first_valid_key = jnp.where(n > 0, 0, blk)
    mask = (rel >= 0) & (rel < WINDOW) & (kj >= first_valid_key)
    qlow = _lane_iota((blk, LANES)) < HEAD_DIM
    contract_lanes = (((1,), (1,)), ((), ()))

    kv_cache = {}

    def kv_group(g):
        if g not in kv_cache:
            c, half = divmod(g, 2)
            kc = kb[:, c * LANES:(c + 1) * LANES]
            vc = vb[:, c * LANES:(c + 1) * LANES]
            kr = pltpu.roll(kc, HEAD_DIM, 1)
            vr = pltpu.roll(vc, HEAD_DIM, 1)
            k2 = (jnp.where(low, kc, kr) if half == 0 else jnp.where(low, kr, kc)).astype(BF16)
            v_lo, v_hi = (vc, vr) if half == 0 else (vr, vc)
            v_even = jnp.where(low, v_lo, 1.0).astype(BF16)
            v_odd = jnp.where(low, 1.0, v_hi).astype(BF16)
            kv_cache[g] = (k2, v_even, v_odd)
        return kv_cache[g]

    for p in range(heads // 2):
        k2 = kv_group((2 * p) // GQA_GROUP)[0]
        qp = q_ref[:, p * LANES:(p + 1) * LANES]
        for par in range(2):
            qh = jnp.where(qlow if par == 0 else ~qlow, qp, jnp.zeros_like(qp))
            s_scr[2 * p + par] = lax.dot_general(qh, k2, contract_lanes, preferred_element_type=F32)

    s = jnp.where(mask[None], s_scr[...], NEG)
    sink = sink_ref[...]
    m = jnp.broadcast_to(jnp.maximum(jnp.max(s, axis=-1, keepdims=True), sink), e_scr.shape)
    p_scr[...] = jnp.exp((s - jnp.concatenate([m] * (2 * blk // LANES), axis=-1)).astype(BF16))
    e_scr[...] = jnp.exp(sink - m)

    for p in range(heads // 2):
        _, v_even, v_odd = kv_group((2 * p) // GQA_GROUP)
        outs = []
        for par in range(2):
            hq = 2 * p + par
            acc = jnp.dot(p_scr[hq], v_even if par == 0 else v_odd, preferred_element_type=F32)
            outs.append(acc / (pltpu.roll(acc, HEAD_DIM, 1) + e_scr[hq]))
        o_ref[:, p * LANES:(p + 1) * LANES] = jnp.where(qlow, outs[0], outs[1]).astype(o_ref.dtype)


def _swa(qkv, sinks, batch, seq):
    t, n = qkv.shape
    heads = sinks.shape[-1]
    nq = heads * HEAD_DIM
    nkv = (n - nq) // 2
    blk = WINDOW
    nb = seq // blk
    kcol = nq // nkv
    return pl.pallas_call(
        functools.partial(_swa_kernel, heads=heads),
        grid=(batch, nb),
        in_specs=[
            pl.BlockSpec((blk, nq), lambda b, i: (b * nb + i, 0)),
            pl.BlockSpec((blk, nkv), lambda b, i: (b * nb + jnp.maximum(i - 1, 0), kcol)),
            pl.BlockSpec((blk, nkv), lambda b, i: (b * nb + i, kcol)),
            pl.BlockSpec((blk, nkv), lambda b, i: (b * nb + jnp.maximum(i - 1, 0), kcol + 1)),
            pl.BlockSpec((blk, nkv), lambda b, i: (b * nb + i, kcol + 1)),
            pl.BlockSpec((heads, 1, 1), lambda b, i: (0, 0, 0)),
        ],
        out_specs=pl.BlockSpec((blk, nq), lambda b, i: (b * nb + i, 0)),
        out_shape=jax.ShapeDtypeStruct((t, nq), BF16),
        scratch_shapes=[
            pltpu.VMEM((heads, blk, 2 * blk), F32),
            pltpu.VMEM((heads, blk, 2 * blk), BF16),
            pltpu.VMEM((heads, blk, LANES), F32),
        ],
        compiler_params=_params("parallel", "parallel"),
        name="swa_attention",
    )(qkv, qkv, qkv, qkv, qkv, sinks.reshape(heads, 1, 1))


AUG_HEADS = LANES // 4
AUG_Q0 = 3 * AUG_HEADS


def _split3(x):
    hi = x.astype(BF16).astype(F32)
    r1 = x - hi
    mid = r1.astype(BF16).astype(F32)
    lo = (r1 - mid).astype(BF16).astype(F32)
    return hi, mid, lo


def _fox_prep_kernel(gl_ref, f_ref, ka_ref, carry_ref):
    @pl.when(pl.program_id(1) == 0)
    def _():
        carry_ref[...] = jnp.zeros_like(carry_ref)

    ts = gl_ref.shape[0]
    logf = _log_sigmoid(gl_ref[...]) * LOG2E
    f = jnp.dot(_tril_f32(ts), logf, preferred_element_type=F32, precision=lax.Precision.HIGHEST) + carry_ref[...]
    f_ref[...] = f
    carry_ref[...] = f[ts - 1:ts, :]

    r = lax.broadcasted_iota(jnp.int32, (LANES, LANES), 0)
    c = lax.broadcasted_iota(jnp.int32, (LANES, LANES), 1)
    aug = jnp.where((_lane_iota(f.shape) >= AUG_Q0) & (_lane_iota(f.shape) < AUG_Q0 + 6), 1.0, 0.0)
    for p, part in enumerate(_split3(-f)):
        place = ((c == 3 * r + p) & (r < AUG_HEADS)).astype(BF16)
        aug = aug + jnp.dot(part.astype(BF16), place, preferred_element_type=F32)
    ka_ref[...] = aug.astype(BF16)


def _fox_prep(gate_logits, batch, seq, ts):
    ns = seq // ts
    return pl.pallas_call(
        _fox_prep_kernel,
        grid=(batch, ns),
        in_specs=[pl.BlockSpec((ts, LANES), lambda b, s: (b * ns + s, 0))],
        out_specs=[pl.BlockSpec((ts, LANES), lambda b, s: (b * ns + s, 0))] * 2,
        out_shape=[
            jax.ShapeDtypeStruct((batch * seq, LANES), F32),
            jax.ShapeDtypeStruct((batch * seq, LANES), BF16),
        ],
        scratch_shapes=[pltpu.VMEM((1, LANES), F32)],
        compiler_params=_params("parallel", "arbitrary"),
        name="fox_gate_cumsum",
    )(gate_logits)


def _fox_kernel(q_ref, k_ref, v_ref, f_ref, ka_ref, o_ref, m_scr, alpha_scr, acc_scr, s_scr, p_scr, ve_scr, vo_scr,
                *, tq, tk):
    hp = pl.program_id(1)
    i = pl.program_id(2)

    @pl.when(i == 0)
    def _():
        v = v_ref[...]
        vlow = _lane_iota(v.shape) < HEAD_DIM
        ones = jnp.ones_like(v)
        ve_scr[...] = jnp.where(vlow, v, ones)
        vo_scr[...] = jnp.where(vlow, ones, v)

    lane = _lane_iota((tq, LANES))
    low = lane < HEAD_DIM
    q = q_ref[...]
    f_tile = f_ref[...]
    contract_lanes = (((1,), (1,)), ((), ()))

    q_aug = []
    for par in range(2):
        head = 2 * hp + par
        fq = jnp.sum(jnp.where(lane == head, f_tile, 0.0), axis=-1, keepdims=True)
        hi, mid, lo = _split3(fq)
        a = jnp.where((lane >= 3 * head) & (lane < 3 * head + 3), 1.0, 0.0)
        a = jnp.where(lane == AUG_Q0 + 3 * par, hi, a)
        a = jnp.where(lane == AUG_Q0 + 3 * par + 1, mid, a)
        a = jnp.where(lane == AUG_Q0 + 3 * par + 2, lo, a)
        qh = jnp.where(low, q, jnp.zeros_like(q)) if par == 0 else jnp.where(low, jnp.zeros_like(q), q)
        q_aug.append(jnp.concatenate([qh, a.astype(BF16)], axis=1))

    m_scr[...] = jnp.full(m_scr.shape, NEG, F32)
    acc_scr[...] = jnp.zeros(acc_scr.shape, F32)
    n_full = (i * tq) // tk

    def tile(j, masked):
        start = pl.multiple_of(j * tk, tk)
        k_aug = jnp.concatenate([k_ref[pl.ds(start, tk), :], ka_ref[pl.ds(start, tk), :]], axis=1)
        v_heads = (ve_scr[pl.ds(start, tk), :], vo_scr[pl.ds(start, tk), :])
        for par in range(2):
            s_scr[par] = lax.dot_general(q_aug[par], k_aug, contract_lanes, preferred_element_type=F32)
        for par in range(2):
            z = s_scr[par]
            if masked:
                r = lax.broadcasted_iota(jnp.int32, z.shape, 0) + i * tq
                c = lax.broadcasted_iota(jnp.int32, z.shape, 1) + start
                z = jnp.where(c <= r, z, NEG)
            m_prev = m_scr[par]
            m_new = jnp.maximum(m_prev, jnp.max(z, axis=-1, keepdims=True))
            alpha_scr[par] = jnp.exp2(m_prev - m_new)
            m_scr[par] = m_new
            p_scr[par] = jnp.exp2((z - jnp.concatenate([m_new] * (tk // LANES), axis=1)).astype(BF16))
        for par in range(2):
            acc_scr[par] = alpha_scr[par] * acc_scr[par] + jnp.dot(p_scr[par], v_heads[par],
                                                                   preferred_element_type=F32)

    def body(j, carry):
        tile(j, False)
        return carry

    lax.fori_loop(0, n_full, body, 0)
    tile(n_full, True)

    a0 = acc_scr[0]
    a1 = acc_scr[1]
    o_ref[...] = jnp.where(low, a0 / pltpu.roll(a0, HEAD_DIM, 1), a1 / pltpu.roll(a1, HEAD_DIM, 1)).astype(o_ref.dtype)


def _fox(qkv, f, k_aug, batch, seq, heads, tk):
    assert heads <= AUG_HEADS
    t = qkv.shape[0]
    tq = _pick(seq, (512, 256, 128))
    nq = seq // tq
    pairs = heads // 2
    return pl.pallas_call(
        functools.partial(_fox_kernel, tq=tq, tk=tk),
        grid=(batch, pairs, nq),
        in_specs=[
            pl.BlockSpec((tq, LANES), lambda b, p, i: (b * nq + i, p)),
            pl.BlockSpec((seq, LANES), lambda b, p, i: (b, pairs + p)),
            pl.BlockSpec((seq, LANES), lambda b, p, i: (b, 2 * pairs + p)),
            pl.BlockSpec((tq, LANES), lambda b, p, i: (b * nq + i, 0)),
            pl.BlockSpec((seq, LANES), lambda b, p, i: (b, 0)),
        ],
        out_specs=pl.BlockSpec((tq, LANES), lambda b, p, i: (b * nq + i, p)),
        out_shape=jax.ShapeDtypeStruct((t, heads * HEAD_DIM), BF16),
        scratch_shapes=[
            pltpu.VMEM((2, tq, LANES), F32),
            pltpu.VMEM((2, tq, LANES), F32),
            pltpu.VMEM((2, tq, LANES), F32),
            pltpu.VMEM((2, tq, tk), F32),
            pltpu.VMEM((2, tq, tk), BF16),
            pltpu.VMEM((seq, LANES), BF16),
            pltpu.VMEM((seq, LANES), BF16),
        ],
        compiler_params=_params("parallel", "parallel", "arbitrary"),
        name="fox_attention",
    )(qkv, qkv, qkv, f, k_aug)


def _mlstm_kernel(q_ref, k_ref, v_ref, o_ref, g_ref, out_ref, c_scr, m_scr, *, dk, dv):
    @pl.when(pl.program_id(1) == 0)
    def _():
        c_scr[...] = jnp.zeros_like(c_scr)
        m_scr[...] = jnp.zeros_like(m_scr)

    L = g_ref.shape[0]
    gates = g_ref[...]
    b_all = jnp.dot(_tril_f32(L), _log_sigmoid(gates), preferred_element_type=F32,
                    precision=lax.Precision.HIGHEST)
    gates_t = gates.T
    b_all_t = b_all.T
    r = lax.broadcasted_iota(jnp.int32, (L, L), 0)
    c = lax.broadcasted_iota(jnp.int32, (L, L), 1)
    tri = c <= r
    ones_col = (_lane_iota((L, LANES)) == 0).astype(BF16)
    contract_lanes = (((1,), (1,)), ((), ()))
    contract_rows = (((0,), (0,)), ((), ()))

    for hd in range(C_HEADS):
        q = q_ref[:, hd * dk:(hd + 1) * dk]
        k = k_ref[:, hd * dk:(hd + 1) * dk]
        v_aug = jnp.concatenate([v_ref[:, hd * dv:(hd + 1) * dv], ones_col], axis=1)
        b_col = b_all[:, C_HEADS + hd:C_HEADS + hd + 1]
        i_col = gates[:, hd:hd + 1]
        b_row = b_all_t[C_HEADS + hd:C_HEADS + hd + 1, :]
        i_row = gates_t[hd:hd + 1, :]
        m_prev = m_scr[hd:hd + 1, 0:1]
        g_last = b_col[L - 1:L, :]

        dm = jnp.where(tri, b_col - b_row + i_row, NEG)
        inter = b_col + m_prev
        m_t = jnp.maximum(inter, jnp.max(dm, axis=-1, keepdims=True))
        a = jnp.exp(dm - m_t) * lax.dot_general(q, k, contract_lanes, preferred_element_type=F32)
        sc = jnp.exp(inter - m_t)
        state = c_scr[hd]
        num = sc * jnp.dot(q, state.astype(BF16), preferred_element_type=F32) \
            + jnp.dot(a.astype(BF16), v_aug, preferred_element_type=F32)
        den = num[:, dv:dv + 1]
        h_t = num[:, :dv] / jnp.maximum(jnp.abs(den), jnp.exp(-m_t))
        gate_o = jax.nn.sigmoid(o_ref[:, hd * dv:(hd + 1) * dv].astype(F32))
        out_ref[:, hd * dv:(hd + 1) * dv] = (gate_o * h_t).astype(out_ref.dtype)

        wlog = g_last - b_col + i_col
        m_new = jnp.maximum(g_last + m_prev, jnp.max(wlog, axis=0, keepdims=True))
        decay = jnp.exp(g_last + m_prev - m_new)
        kw = (k.astype(F32) * jnp.exp(wlog - m_new)).astype(BF16)
        c_scr[hd] = decay * state + lax.dot_general(kw, v_aug, contract_rows, preferred_element_type=F32)
        m_scr[hd:hd + 1, :] = jnp.broadcast_to(m_new, (1, LANES))


def _mlstm(proj, gates, batch, seq):
    t, n = proj.shape
    dv = n // (3 * C_HEADS)
    dk = dv // 2
    L = _pick(seq, (256, 128, 64))
    nc = seq // L
    hq = C_HEADS * dk
    hv = C_HEADS * dv
    qb, vb = hq // hq, (2 * hq) // hv
    return pl.pallas_call(
        functools.partial(_mlstm_kernel, dk=dk, dv=dv),
        grid=(batch, nc),
        in_specs=[
            pl.BlockSpec((L, hq), lambda b, s: (b * nc + s, 0)),
            pl.BlockSpec((L, hq), lambda b, s: (b * nc + s, qb)),
            pl.BlockSpec((L, hv), lambda b, s: (b * nc + s, vb)),
            pl.BlockSpec((L, hv), lambda b, s: (b * nc + s, vb + 1)),
            pl.BlockSpec((L, LANES), lambda b, s: (b * nc + s, 0)),
        ],
        out_specs=pl.BlockSpec((L, hv), lambda b, s: (b * nc + s, 0)),
        out_shape=jax.ShapeDtypeStruct((t, hv), BF16),
        scratch_shapes=[pltpu.VMEM((C_HEADS, dk, dv + LANES), F32), pltpu.VMEM((SUBLANES, LANES), F32)],
        compiler_params=_params("parallel", "arbitrary"),
        name="mlstm",
    )(proj, proj, proj, proj, gates)


def _final_norm_kernel(x_ref, g_ref, o_ref):
    x = x_ref[...]
    o_ref[...] = x * lax.rsqrt(jnp.mean(x * x, axis=-1, keepdims=True) + EPS) * g_ref[...]


def _final_norm(x, g):
    t, d = x.shape
    tm = _pick(t, (1024, 512, 256, 128))
    return pl.pallas_call(
        _final_norm_kernel,
        grid=(t // tm,),
        in_specs=[pl.BlockSpec((tm, d), lambda i: (i, 0)), pl.BlockSpec((1, d), lambda i: (0, 0))],
        out_specs=pl.BlockSpec((tm, d), lambda i: (i, 0)),
        out_shape=jax.ShapeDtypeStruct((t, d), F32),
        compiler_params=_params("parallel"),
        name="final_norm",
    )(x, g)


def _split_gate_cols(w_in, bias, n_main):
    n_gate = w_in.shape[1] - n_main
    wg = jnp.zeros((w_in.shape[0], LANES), BF16).at[:, :n_gate].set(w_in[:, n_main:].astype(BF16))
    bg = jnp.zeros((1, LANES), F32).at[0, :n_gate].set(bias)
    return w_in[:, :n_main].astype(BF16), wg, bg


def kernel(x, c, positions, mod_w, mod_b, norm_g, final_g, a_w_in, a_sinks, a_w_out, b_w_in, b_f_bias, b_w_out, c_w_in, c_gate_bias, c_w_out, ffn_w_up, ffn_conv_w, ffn_conv_b, ffn_w_down):
    batch, seq, d = x.shape
    depth = mod_w.shape[0]
    t = batch * seq
    xf = x.reshape(t, d)

    mod = _modulation(c, mod_w, mod_b)[:, :batch, :].reshape(depth, batch, 6, 1, d)
    cos, sins = _rope_tables(positions)

    for i in range(depth):
        sh1, sc1, g1, sh2, sc2, g2 = (mod[i, :, r] for r in range(6))
        gain1 = norm_g[i, 0].reshape(1, d)
        gain2 = norm_g[i, 1].reshape(1, d)
        kind, j = i % N_MIXERS, i // N_MIXERS
        if kind == 0:
            qkv = _proj_a(xf, gain1, sh1, sc1, a_w_in[j].astype(BF16), cos, sins, seq)
            y = _swa(qkv, a_sinks[j], batch, seq)
            w_out = a_w_out[j]
        elif kind == 1:
            heads = b_f_bias.shape[-1]
            n_main = 3 * heads * HEAD_DIM
            w, wg, bg = _split_gate_cols(b_w_in[j], b_f_bias[j], n_main)
            qkv, gl = _proj_gate(xf, gain1, sh1, sc1, w, wg, bg, seq, heads * HEAD_DIM, HEAD_DIM ** -0.5 * LOG2E)
            tk = _pick(seq, (512, 256, 128))
            f, k_aug = _fox_prep(gl, batch, seq, tk)
            y = _fox(qkv, f, k_aug, batch, seq, heads, tk)
            w_out = b_w_out[j]
        else:
            n_main = c_w_in.shape[-1] - 2 * C_HEADS
            dk = n_main // (6 * C_HEADS)
            w, wg, bg = _split_gate_cols(c_w_in[j], c_gate_bias[j], n_main)
            proj, gl = _proj_gate(xf, gain1, sh1, sc1, w, wg, bg, seq, C_HEADS * dk, dk ** -0.5)
            y = _mlstm(proj, gl, batch, seq)
            w_out = c_w_out[j]
        xf = _proj_res(y, w_out.astype(BF16), xf, g1, seq)

        conv_tab = jnp.zeros((SUBLANES, ffn_conv_w.shape[-1]), F32)
        conv_tab = conv_tab.at[:CONV_WIDTH].set(ffn_conv_w[i]).at[CONV_WIDTH].set(ffn_conv_b[i])
        act = _ffn_up(xf, gain2, sh2, sc2, ffn_w_up[i].astype(BF16), conv_tab, seq)
        xf = _proj_res(act, ffn_w_down[i].astype(BF16), xf, g2, seq)

    return _final_norm(xf, final_g.reshape(1, d)).reshape(batch, seq, d)
```

```python
import functools

import jax
import jax.numpy as jnp
from jax import lax
from jax.experimental import pallas as pl
from jax.experimental.pallas import tpu as pltpu

F32 = jnp.float32
BF16 = jnp.bfloat16

EPS = 1e-6
NEG = -1e30
ROPE_THETA = 10000.0
HEAD_DIM = 64
HALF = HEAD_DIM // 2
GQA_GROUP = 8
WINDOW = 128
C_HEADS = 4
N_MIXERS = 3
CONV_WIDTH = 3
LANES = 128
SUBLANES = 8
LOG2E = 1.4426950408889634
VMEM_LIMIT = 56 * 1024 * 1024


def _pick(dim, prefs):
    for p in prefs:
        if dim % p == 0:
            return p
    return dim


def _params(*sem):
    return pltpu.CompilerParams(dimension_semantics=sem, vmem_limit_bytes=VMEM_LIMIT)


def _lane_iota(shape):
    return lax.broadcasted_iota(jnp.int32, shape, len(shape) - 1)


def _log_sigmoid(x):
    return jnp.minimum(x, 0.0) - jnp.log1p(jnp.exp(-jnp.abs(x)))


def _tril_f32(n):
    r = lax.broadcasted_iota(jnp.int32, (n, n), 0)
    c = lax.broadcasted_iota(jnp.int32, (n, n), 1)
    return (c <= r).astype(F32)


def _mod_kernel(c_ref, w_ref, b_ref, o_ref):
    c = c_ref[...]
    act = (c * jax.nn.sigmoid(c)).astype(BF16)
    o_ref[...] = jnp.dot(act, w_ref[...].astype(BF16), preferred_element_type=F32) + b_ref[...]


def _modulation(c, mod_w, mod_b):
    depth, d, n = mod_w.shape
    b = c.shape[0]
    cp = jnp.zeros((SUBLANES, d), F32).at[:b].set(c)
    tn = _pick(n, (1024, 512, 256, 128))
    return pl.pallas_call(
        _mod_kernel,
        grid=(depth, n // tn),
        in_specs=[
            pl.BlockSpec((SUBLANES, d), lambda l, j: (0, 0)),
            pl.BlockSpec((None, d, tn), lambda l, j: (l, 0, j)),
            pl.BlockSpec((None, 1, tn), lambda l, j: (l, 0, j)),
        ],
        out_specs=pl.BlockSpec((None, SUBLANES, tn), lambda l, j: (l, 0, j)),
        out_shape=jax.ShapeDtypeStruct((depth, SUBLANES, n), F32),
        compiler_params=_params("parallel", "parallel"),
        name="modulation",
    )(cp, mod_w, mod_b.reshape(depth, 1, n))


def _rope_table_kernel(pos_ref, freq_ref, cos_ref, sin_ref):
    ang = pos_ref[...] * freq_ref[...]
    lane = _lane_iota(ang.shape)
    cos_ref[...] = jnp.cos(ang)
    s = jnp.sin(ang)
    sin_ref[...] = jnp.where((lane & (HEAD_DIM - 1)) < HALF, -s, s)


def _rope_tables(positions):
    t = positions.size
    pos = positions.reshape(t, 1).astype(F32)
    inv_freq = ROPE_THETA ** (-jnp.arange(0, HEAD_DIM, 2, dtype=F32) / HEAD_DIM)
    freq = jnp.tile(inv_freq, LANES // HALF).reshape(1, LANES)
    ts = _pick(t, (1024, 512, 256, 128))
    return pl.pallas_call(
        _rope_table_kernel,
        grid=(t // ts,),
        in_specs=[pl.BlockSpec((ts, 1), lambda i: (i, 0)), pl.BlockSpec((1, LANES), lambda i: (0, 0))],
        out_specs=[pl.BlockSpec((ts, LANES), lambda i: (i, 0))] * 2,
        out_shape=[jax.ShapeDtypeStruct((t, LANES), F32)] * 2,
        compiler_params=_params("parallel"),
        name="rope_tables",
    )(pos, freq)


def _norm_mod(x_ref, g_ref, sh_ref, sc_ref):
    x = x_ref[...]
    y = x * lax.rsqrt(jnp.mean(x * x, axis=-1, keepdims=True) + EPS)
    y = y * g_ref[...]
    return (y * (1.0 + sc_ref[...]) + sh_ref[...]).astype(BF16)


def _rope(a, cos, sins):
    lane = _lane_iota(a.shape)
    swapped = jnp.where((lane & (HEAD_DIM - 1)) < HALF, pltpu.roll(a, LANES - HALF, 1), pltpu.roll(a, HALF, 1))
    return a * cos + swapped * sins


def _proj_a_kernel(x_ref, g_ref, sh_ref, sc_ref, w_ref, cos_ref, sin_ref, o_ref, *, nq, nkv):
    h = _norm_mod(x_ref, g_ref, sh_ref, sc_ref)
    acc = jnp.dot(h, w_ref[...], preferred_element_type=F32)
    cos = cos_ref[...]
    sins = sin_ref[...]
    for c in range((nq + nkv) // LANES):
        a = _rope(acc[:, c * LANES:(c + 1) * LANES], cos, sins)
        if c * LANES < nq:
            a = a * (HEAD_DIM ** -0.5)
        o_ref[:, c * LANES:(c + 1) * LANES] = a.astype(BF16)
    o_ref[:, nq + nkv:] = acc[:, nq + nkv:].astype(BF16)


def _proj_a(x, g, sh, sc, w, cos, sins, seq):
    t, d = x.shape
    n = w.shape[1]
    nq = d
    nkv = (n - nq) // 2
    tm = _pick(seq, (512, 256, 128))
    per_b = seq // tm
    return pl.pallas_call(
        functools.partial(_proj_a_kernel, nq=nq, nkv=nkv),
        grid=(t // tm,),
        in_specs=[
            pl.BlockSpec((tm, d), lambda i: (i, 0)),
            pl.BlockSpec((1, d), lambda i: (0, 0)),
            pl.BlockSpec((None, 1, d), lambda i: (i // per_b, 0, 0)),
            pl.BlockSpec((None, 1, d), lambda i: (i // per_b, 0, 0)),
            pl.BlockSpec((d, n), lambda i: (0, 0)),
            pl.BlockSpec((tm, LANES), lambda i: (i, 0)),
            pl.BlockSpec((tm, LANES), lambda i: (i, 0)),
        ],
        out_specs=pl.BlockSpec((tm, n), lambda i: (i, 0)),
        out_shape=jax.ShapeDtypeStruct((t, n), BF16),
        compiler_params=_params("parallel"),
        name="proj_swa",
    )(x, g, sh, sc, w, cos, sins)


def _proj_gate_kernel(x_ref, g_ref, sh_ref, sc_ref, w_ref, wg_ref, bg_ref, o_ref, og_ref, h_scr, *,
                      q_tiles, q_scale):
    j = pl.program_id(1)

    @pl.when(j == 0)
    def _():
        h_scr[...] = _norm_mod(x_ref, g_ref, sh_ref, sc_ref)
        og_ref[...] = jnp.dot(h_scr[...], wg_ref[...], preferred_element_type=F32) + bg_ref[...]

    acc = jnp.dot(h_scr[...], w_ref[...], preferred_element_type=F32)
    o_ref[...] = (acc * jnp.where(j < q_tiles, q_scale, 1.0)).astype(BF16)


def _proj_gate(x, g, sh, sc, w, wg, bg, seq, q_cols, q_scale):
    t, d = x.shape
    n = w.shape[1]
    tm = _pick(seq, (1024, 512, 256, 128))
    tn = _pick(q_cols, (1024, 512, 256, 128))
    per_b = seq // tm
    return pl.pallas_call(
        functools.partial(_proj_gate_kernel, q_tiles=q_cols // tn, q_scale=q_scale),
        grid=(t // tm, n // tn),
        in_specs=[
            pl.BlockSpec((tm, d), lambda i, j: (i, 0)),
            pl.BlockSpec((1, d), lambda i, j: (0, 0)),
            pl.BlockSpec((None, 1, d), lambda i, j: (i // per_b, 0, 0)),
            pl.BlockSpec((None, 1, d), lambda i, j: (i // per_b, 0, 0)),
            pl.BlockSpec((d, tn), lambda i, j: (0, j)),
            pl.BlockSpec((d, LANES), lambda i, j: (0, 0)),
            pl.BlockSpec((1, LANES), lambda i, j: (0, 0)),
        ],
        out_specs=[pl.BlockSpec((tm, tn), lambda i, j: (i, j)), pl.BlockSpec((tm, LANES), lambda i, j: (i, 0))],
        out_shape=[jax.ShapeDtypeStruct((t, n), BF16), jax.ShapeDtypeStruct((t, LANES), F32)],
        scratch_shapes=[pltpu.VMEM((tm, d), BF16)],
        compiler_params=_params("parallel", "arbitrary"),
        name="proj_gate",
    )(x, g, sh, sc, w, wg, bg)


def _proj_res_kernel(y_ref, w_ref, x_ref, g_ref, o_ref):
    acc = jnp.dot(y_ref[...], w_ref[...], preferred_element_type=F32)
    o_ref[...] = x_ref[...] + g_ref[...] * acc


def _proj_res(y, w, x, gate, seq):
    t, k = y.shape
    n = w.shape[1]
    tm = _pick(seq, (1024, 512, 256, 128))
    tn = _pick(n, (1024, 512, 256, 128) if k <= n else (512, 256, 128))
    per_b = seq // tm
    return pl.pallas_call(
        _proj_res_kernel,
        grid=(t // tm, n // tn),
        in_specs=[
            pl.BlockSpec((tm, k), lambda i, j: (i, 0)),
            pl.BlockSpec((k, tn), lambda i, j: (0, j)),
            pl.BlockSpec((tm, tn), lambda i, j: (i, j)),
            pl.BlockSpec((None, 1, tn), lambda i, j: (i // per_b, 0, j)),
        ],
        out_specs=pl.BlockSpec((tm, tn), lambda i, j: (i, j)),
        out_shape=jax.ShapeDtypeStruct((t, n), F32),
        input_output_aliases={2: 0},
        compiler_params=_params("parallel", "arbitrary"),
        name="proj_residual",
    )(y, w, x, gate)


def _ffn_up_kernel(x_ref, g_ref, sh_ref, sc_ref, wg_ref, wv_ref, cg_ref, cv_ref, o_ref, h_scr, tg_scr, tv_scr, *,
                   per_b):
    i = pl.program_id(0)
    j = pl.program_id(1)
    tm = o_ref.shape[0]

    @pl.when(j == 0)
    def _():
        h_scr[...] = _norm_mod(x_ref, g_ref, sh_ref, sc_ref)

    @pl.when(i == 0)
    def _():
        tg_scr[j] = jnp.zeros(tg_scr.shape[1:], F32)
        tv_scr[j] = jnp.zeros(tv_scr.shape[1:], F32)

    h = h_scr[...]
    ug = jnp.dot(h, wg_ref[...], preferred_element_type=F32)
    uv = jnp.dot(h, wv_ref[...], preferred_element_type=F32)

    def conv(u, c_ref, shifted1, shifted2):
        return u * c_ref[2:3, :] + shifted1 * c_ref[1:2, :] + shifted2 * c_ref[0:1, :] + c_ref[3:4, :]

    def glu(yg, yv):
        return (yg * jax.nn.sigmoid(yg) * yv).astype(o_ref.dtype)

    yg = conv(ug, cg_ref, pltpu.roll(ug, 1, 0), pltpu.roll(ug, 2, 0))
    yv = conv(uv, cv_ref, pltpu.roll(uv, 1, 0), pltpu.roll(uv, 2, 0))
    o_ref[...] = glu(yg, yv)

    first = (i % per_b) == 0
    row = lax.broadcasted_iota(jnp.int32, (SUBLANES, ug.shape[1]), 0)

    def head_rows(u, t_scr, c_ref):
        top = u[0:SUBLANES, :]
        tail = jnp.where(first, 0.0, t_scr[j])
        s1 = jnp.where(row == 0, pltpu.roll(tail, 1, 0), pltpu.roll(top, 1, 0))
        s2 = jnp.where(row <= 1, pltpu.roll(tail, 2, 0), pltpu.roll(top, 2, 0))
        t_scr[j] = u[tm - SUBLANES:tm, :]
        return conv(top, c_ref, s1, s2)

    o_ref[0:SUBLANES, :] = glu(head_rows(ug, tg_scr, cg_ref), head_rows(uv, tv_scr, cv_ref))


def _ffn_up(x, g, sh, sc, w_up, conv_tab, seq):
    t, d = x.shape
    f = w_up.shape[1] // 2
    tm = _pick(seq, (1024, 512, 256, 128))
    tn = _pick(f, (512, 256, 128))
    nj = f // tn
    per_b = seq // tm
    return pl.pallas_call(
        functools.partial(_ffn_up_kernel, per_b=per_b),
        grid=(t // tm, nj),
        in_specs=[
            pl.BlockSpec((tm, d), lambda i, j: (i, 0)),
            pl.BlockSpec((1, d), lambda i, j: (0, 0)),
            pl.BlockSpec((None, 1, d), lambda i, j: (i // per_b, 0, 0)),
            pl.BlockSpec((None, 1, d), lambda i, j: (i // per_b, 0, 0)),
            pl.BlockSpec((d, tn), lambda i, j: (0, j)),
            pl.BlockSpec((d, tn), lambda i, j: (0, j + nj)),
            pl.BlockSpec((SUBLANES, tn), lambda i, j: (0, j)),
            pl.BlockSpec((SUBLANES, tn), lambda i, j: (0, j + nj)),
        ],
        out_specs=pl.BlockSpec((tm, tn), lambda i, j: (i, j)),
        out_shape=jax.ShapeDtypeStruct((t, f), BF16),
        scratch_shapes=[
            pltpu.VMEM((tm, d), BF16),
            pltpu.VMEM((nj, SUBLANES, tn), F32),
            pltpu.VMEM((nj, SUBLANES, tn), F32),
        ],
        compiler_params=_params("arbitrary", "arbitrary"),
        name="ffn_up_conv_glu",
    )(x, g, sh, sc, w_up, w_up, conv_tab, conv_tab)


def _swa_kernel(q_ref, kp_ref, kc_ref, vp_ref, vc_ref, sink_ref, o_ref, s_scr, p_scr, e_scr, *, heads):
    n = pl.program_id(1)
    blk = q_ref.shape[0]
    kb = jnp.concatenate([kp_ref[...], kc_ref[...]], axis=0).astype(F32)
    vb = jnp.concatenate([vp_ref[...], vc_ref[...]], axis=0).astype(F32)
    lane = _lane_iota((2 * blk, LANES))
    low = lane < HEAD_DIM
    qi = lax.broadcasted_iota(jnp.int32, (blk, 2 * blk), 0)
    kj = lax.broadcasted_iota(jnp.int32, (blk, 2 * blk), 1)
    rel = qi + blk - kj
    first_valid_key = jnp.where(n > 0, 0, blk)
    mask = (rel >= 0) & (rel < WINDOW) & (kj >= first_valid_key)
    qlow = _lane_iota((blk, LANES)) < HEAD_DIM
    contract_lanes = (((1,), (1,)), ((), ()))

    kv_cache = {}

    def kv_group(g):
        if g not in kv_cache:
            c, half = divmod(g, 2)
            kc = kb[:, c * LANES:(c + 1) * LANES]
            vc = vb[:, c * LANES:(c + 1) * LANES]
            kr = pltpu.roll(kc, HEAD_DIM, 1)
            vr = pltpu.roll(vc, HEAD_DIM, 1)
            k2 = (jnp.where(low, kc, kr) if half == 0 else jnp.where(low, kr, kc)).astype(BF16)
            v_lo, v_hi = (vc, vr) if half == 0 else (vr, vc)
            v_even = jnp.where(low, v_lo, 1.0).astype(BF16)
            v_odd = jnp.where(low, 1.0, v_hi).astype(BF16)
            kv_cache[g] = (k2, v_even, v_odd)
        return kv_cache[g]

    for p in range(heads // 2):
        k2 = kv_group((2 * p) // GQA_GROUP)[0]
        qp = q_ref[:, p * LANES:(p + 1) * LANES]
        for par in range(2):
            qh = jnp.where(qlow if par == 0 else ~qlow, qp, jnp.zeros_like(qp))
            s_scr[2 * p + par] = lax.dot_general(qh, k2, contract_lanes, preferred_element_type=F32)

    s = jnp.where(mask[None], s_scr[...], NEG)
    sink = sink_ref[...]
    m = jnp.broadcast_to(jnp.maximum(jnp.max(s, axis=-1, keepdims=True), sink), e_scr.shape)
    p_scr[...] = jnp.exp((s - jnp.concatenate([m] * (2 * blk // LANES), axis=-1)).astype(BF16))
    e_scr[...] = jnp.exp(sink - m)

    for p in range(heads // 2):
        _, v_even, v_odd = kv_group((2 * p) // GQA_GROUP)
        outs = []
        for par in range(2):
            hq = 2 * p + par
            acc = jnp.dot(p_scr[hq], v_even if par == 0 else v_odd, preferred_element_type=F32)
            outs.append(acc / (pltpu.roll(acc, HEAD_DIM, 1) + e_scr[hq]))
        o_ref[:, p * LANES:(p + 1) * LANES] = jnp.where(qlow, outs[0], outs[1]).astype(o_ref.dtype)


def _swa(qkv, sinks, batch, seq):
    t, n = qkv.shape
    heads = sinks.shape[-1]
    nq = heads * HEAD_DIM
    nkv = (n - nq) // 2
    blk = WINDOW
    nb = seq // blk
    kcol = nq // nkv
    return pl.pallas_call(
        functools.partial(_swa_kernel, heads=heads),
        grid=(batch, nb),
        in_specs=[
            pl.BlockSpec((blk, nq), lambda b, i: (b * nb + i, 0)),
            pl.BlockSpec((blk, nkv), lambda b, i: (b * nb + jnp.maximum(i - 1, 0), kcol)),
            pl.BlockSpec((blk, nkv), lambda b, i: (b * nb + i, kcol)),
            pl.BlockSpec((blk, nkv), lambda b, i: (b * nb + jnp.maximum(i - 1, 0), kcol + 1)),
            pl.BlockSpec((blk, nkv), lambda b, i: (b * nb + i, kcol + 1)),
            pl.BlockSpec((heads, 1, 1), lambda b, i: (0, 0, 0)),
        ],
        out_specs=pl.BlockSpec((blk, nq), lambda b, i: (b * nb + i, 0)),
        out_shape=jax.ShapeDtypeStruct((t, nq), BF16),
        scratch_shapes=[
            pltpu.VMEM((heads, blk, 2 * blk), F32),
            pltpu.VMEM((heads, blk, 2 * blk), BF16),
            pltpu.VMEM((heads, blk, LANES), F32),
        ],
        compiler_params=_params("parallel", "parallel"),
        name="swa_attention",
    )(qkv, qkv, qkv, qkv, qkv, sinks.reshape(heads, 1, 1))


AUG_HEADS = LANES // 4
AUG_Q0 = 3 * AUG_HEADS


def _split3(x):
    hi = x.astype(BF16).astype(F32)
    r1 = x - hi
    mid = r1.astype(BF16).astype(F32)
    lo = (r1 - mid).astype(BF16).astype(F32)
    return hi, mid, lo


def _fox_prep_kernel(gl_ref, f_ref, ka_ref, carry_ref):
    @pl.when(pl.program_id(1) == 0)
    def _():
        carry_ref[...] = jnp.zeros_like(carry_ref)

    ts = gl_ref.shape[0]
    logf = _log_sigmoid(gl_ref[...]) * LOG2E
    f = jnp.dot(_tril_f32(ts), logf, preferred_element_type=F32, precision=lax.Precision.HIGHEST) + carry_ref[...]
    f_ref[...] = f
    carry_ref[...] = f[ts - 1:ts, :]

    r = lax.broadcasted_iota(jnp.int32, (LANES, LANES), 0)
    c = lax.broadcasted_iota(jnp.int32, (LANES, LANES), 1)
    aug = jnp.where((_lane_iota(f.shape) >= AUG_Q0) & (_lane_iota(f.shape) < AUG_Q0 + 6), 1.0, 0.0)
    for p, part in enumerate(_split3(-f)):
        place = ((c == 3 * r + p) & (r < AUG_HEADS)).astype(BF16)
        aug = aug + jnp.dot(part.astype(BF16), place, preferred_element_type=F32)
    ka_ref[...] = aug.astype(BF16)


def _fox_prep(gate_logits, batch, seq, ts):
    ns = seq // ts
    return pl.pallas_call(
        _fox_prep_kernel,
        grid=(batch, ns),
        in_specs=[pl.BlockSpec((ts, LANES), lambda b, s: (b * ns + s, 0))],
        out_specs=[pl.BlockSpec((ts, LANES), lambda b, s: (b * ns + s, 0))] * 2,
        out_shape=[
            jax.ShapeDtypeStruct((batch * seq, LANES), F32),
            jax.ShapeDtypeStruct((batch * seq, LANES), BF16),
        ],
        scratch_shapes=[pltpu.VMEM((1, LANES), F32)],
        compiler_params=_params("parallel", "arbitrary"),
        name="fox_gate_cumsum",
    )(gate_logits)


def _fox_kernel(q_ref, k_ref, v_ref, f_ref, ka_ref, o_ref, m_scr, alpha_scr, acc_scr, s_scr, p_scr, ve_scr, vo_scr,
                *, tq, tk):
    hp = pl.program_id(1)
    i = pl.program_id(2)

    @pl.when(i == 0)
    def _():
        v = v_ref[...]
        vlow = _lane_iota(v.shape) < HEAD_DIM
        ones = jnp.ones_like(v)
        ve_scr[...] = jnp.where(vlow, v, ones)
        vo_scr[...] = jnp.where(vlow, ones, v)

    lane = _lane_iota((tq, LANES))
    low = lane < HEAD_DIM
    q = q_ref[...]
    f_tile = f_ref[...]
    contract_lanes = (((1,), (1,)), ((), ()))

    q_aug = []
    for par in range(2):
        head = 2 * hp + par
        fq = jnp.sum(jnp.where(lane == head, f_tile, 0.0), axis=-1, keepdims=True)
        hi, mid, lo = _split3(fq)
        a = jnp.where((lane >= 3 * head) & (lane < 3 * head + 3), 1.0, 0.0)
        a = jnp.where(lane == AUG_Q0 + 3 * par, hi, a)
        a = jnp.where(lane == AUG_Q0 + 3 * par + 1, mid, a)
        a = jnp.where(lane == AUG_Q0 + 3 * par + 2, lo, a)
        qh = jnp.where(low, q, jnp.zeros_like(q)) if par == 0 else jnp.where(low, jnp.zeros_like(q), q)
        q_aug.append(jnp.concatenate([qh, a.astype(BF16)], axis=1))

    m_scr[...] = jnp.full(m_scr.shape, NEG, F32)
    acc_scr[...] = jnp.zeros(acc_scr.shape, F32)
    n_full = (i * tq) // tk

    def tile(j, masked):
        start = pl.multiple_of(j * tk, tk)
        k_aug = jnp.concatenate([k_ref[pl.ds(start, tk), :], ka_ref[pl.ds(start, tk), :]], axis=1)
        v_heads = (ve_scr[pl.ds(start, tk), :], vo_scr[pl.ds(start, tk), :])
        for par in range(2):
            s_scr[par] = lax.dot_general(q_aug[par], k_aug, contract_lanes, preferred_element_type=F32)
        for par in range(2):
            z = s_scr[par]
            if masked:
                r = lax.broadcasted_iota(jnp.int32, z.shape, 0) + i * tq
                c = lax.broadcasted_iota(jnp.int32, z.shape, 1) + start
                z = jnp.where(c <= r, z, NEG)
            m_prev = m_scr[par]
            m_new = jnp.maximum(m_prev, jnp.max(z, axis=-1, keepdims=True))
            alpha_scr[par] = jnp.exp2(m_prev - m_new)
            m_scr[par] = m_new
            p_scr[par] = jnp.exp2((z - jnp.concatenate([m_new] * (tk // LANES), axis=1)).astype(BF16))
        for par in range(2):
            acc_scr[par] = alpha_scr[par] * acc_scr[par] + jnp.dot(p_scr[par], v_heads[par],
                                                                   preferred_element_type=F32)

    def body(j, carry):
        tile(j, False)
        return carry

    lax.fori_loop(0, n_full, body, 0)
    for d in range(max(tq // tk, 1)):
        tile(n_full + d, True)

    a0 = acc_scr[0]
    a1 = acc_scr[1]
    o_ref[...] = jnp.where(low, a0 / pltpu.roll(a0, HEAD_DIM, 1), a1 / pltpu.roll(a1, HEAD_DIM, 1)).astype(o_ref.dtype)


def _fox(qkv, f, k_aug, batch, seq, heads, tk):
    assert heads <= AUG_HEADS
    t = qkv.shape[0]
    tq = _pick(seq, (1024, 512, 256, 128))
    nq = seq // tq
    pairs = heads // 2
    return pl.pallas_call(
        functools.partial(_fox_kernel, tq=tq, tk=tk),
        grid=(batch, pairs, nq),
        in_specs=[
            pl.BlockSpec((tq, LANES), lambda b, p, i: (b * nq + i, p)),
            pl.BlockSpec((seq, LANES), lambda b, p, i: (b, pairs + p)),
            pl.BlockSpec((seq, LANES), lambda b, p, i: (b, 2 * pairs + p)),
            pl.BlockSpec((tq, LANES), lambda b, p, i: (b * nq + i, 0)),
            pl.BlockSpec((seq, LANES), lambda b, p, i: (b, 0)),
        ],
        out_specs=pl.BlockSpec((tq, LANES), lambda b, p, i: (b * nq + i, p)),
        out_shape=jax.ShapeDtypeStruct((t, heads * HEAD_DIM), BF16),
        scratch_shapes=[
            pltpu.VMEM((2, tq, LANES), F32),
            pltpu.VMEM((2, tq, LANES), F32),
            pltpu.VMEM((2, tq, LANES), F32),
            pltpu.VMEM((2, tq, tk), F32),
            pltpu.VMEM((2, tq, tk), BF16),
            pltpu.VMEM((seq, LANES), BF16),
            pltpu.VMEM((seq, LANES), BF16),
        ],
        compiler_params=_params("parallel", "parallel", "arbitrary"),
        name="fox_attention",
    )(qkv, qkv, qkv, f, k_aug)


def _mlstm_kernel(q_ref, k_ref, v_ref, o_ref, g_ref, out_ref, c_scr, m_scr, *, dk, dv):
    @pl.when(pl.program_id(1) == 0)
    def _():
        c_scr[...] = jnp.zeros_like(c_scr)
        m_scr[...] = jnp.zeros_like(m_scr)

    L = g_ref.shape[0]
    gates = g_ref[...]
    b_all = jnp.dot(_tril_f32(L), _log_sigmoid(gates), preferred_element_type=F32,
                    precision=lax.Precision.HIGHEST)
    gates_t = gates.T
    b_all_t = b_all.T
    r = lax.broadcasted_iota(jnp.int32, (L, L), 0)
    c = lax.broadcasted_iota(jnp.int32, (L, L), 1)
    tri = c <= r
    ones_col = (_lane_iota((L, LANES)) == 0).astype(BF16)
    contract_lanes = (((1,), (1,)), ((), ()))
    contract_rows = (((0,), (0,)), ((), ()))

    for hd in range(C_HEADS):
        q = q_ref[:, hd * dk:(hd + 1) * dk]
        k = k_ref[:, hd * dk:(hd + 1) * dk]
        v_aug = jnp.concatenate([v_ref[:, hd * dv:(hd + 1) * dv], ones_col], axis=1)
        b_col = b_all[:, C_HEADS + hd:C_HEADS + hd + 1]
        i_col = gates[:, hd:hd + 1]
        b_row = b_all_t[C_HEADS + hd:C_HEADS + hd + 1, :]
        i_row = gates_t[hd:hd + 1, :]
        m_prev = m_scr[hd:hd + 1, 0:1]
        g_last = b_col[L - 1:L, :]

        dm = jnp.where(tri, b_col - b_row + i_row, NEG)
        inter = b_col + m_prev
        m_t = jnp.maximum(inter, jnp.max(dm, axis=-1, keepdims=True))
        a = jnp.exp(dm - m_t) * lax.dot_general(q, k, contract_lanes, preferred_element_type=F32)
        sc = jnp.exp(inter - m_t)
        state = c_scr[hd]
        num = sc * jnp.dot(q, state.astype(BF16), preferred_element_type=F32) \
            + jnp.dot(a.astype(BF16), v_aug, preferred_element_type=F32)
        den = num[:, dv:dv + 1]
        h_t = num[:, :dv] / jnp.maximum(jnp.abs(den), jnp.exp(-m_t))
        gate_o = jax.nn.sigmoid(o_ref[:, hd * dv:(hd + 1) * dv].astype(F32))
        out_ref[:, hd * dv:(hd + 1) * dv] = (gate_o * h_t).astype(out_ref.dtype)

        wlog = g_last - b_col + i_col
        m_new = jnp.maximum(g_last + m_prev, jnp.max(wlog, axis=0, keepdims=True))
        decay = jnp.exp(g_last + m_prev - m_new)
        kw = (k.astype(F32) * jnp.exp(wlog - m_new)).astype(BF16)
        c_scr[hd] = decay * state + lax.dot_general(kw, v_aug, contract_rows, preferred_element_type=F32)
        m_scr[hd:hd + 1, :] = jnp.broadcast_to(m_new, (1, LANES))


def _mlstm(proj, gates, batch, seq):
    t, n = proj.shape
    dv = n // (3 * C_HEADS)
    dk = dv // 2
    L = _pick(seq, (256, 128, 64))
    nc = seq // L
    hq = C_HEADS * dk
    hv = C_HEADS * dv
    qb, vb = hq // hq, (2 * hq) // hv
    return pl.pallas_call(
        functools.partial(_mlstm_kernel, dk=dk, dv=dv),
        grid=(batch, nc),
        in_specs=[
            pl.BlockSpec((L, hq), lambda b, s: (b * nc + s, 0)),
            pl.BlockSpec((L, hq), lambda b, s: (b * nc + s, qb)),
            pl.BlockSpec((L, hv), lambda b, s: (b * nc + s, vb)),
            pl.BlockSpec((L, hv), lambda b, s: (b * nc + s, vb + 1)),
            pl.BlockSpec((L, LANES), lambda b, s: (b * nc + s, 0)),
        ],
        out_specs=pl.BlockSpec((L, hv), lambda b, s: (b * nc + s, 0)),
        out_shape=jax.ShapeDtypeStruct((t, hv), BF16),
        scratch_shapes=[pltpu.VMEM((C_HEADS, dk, dv + LANES), F32), pltpu.VMEM((SUBLANES, LANES), F32)],
        compiler_params=_params("parallel", "arbitrary"),
        name="mlstm",
    )(proj, proj, proj, proj, gates)


def _final_norm_kernel(x_ref, g_ref, o_ref):
    x = x_ref[...]
    o_ref[...] = x * lax.rsqrt(jnp.mean(x * x, axis=-1, keepdims=True) + EPS) * g_ref[...]


def _final_norm(x, g):
    t, d = x.shape
    tm = _pick(t, (1024, 512, 256, 128))
    return pl.pallas_call(
        _final_norm_kernel,
        grid=(t // tm,),
        in_specs=[pl.BlockSpec((tm, d), lambda i: (i, 0)), pl.BlockSpec((1, d), lambda i: (0, 0))],
        out_specs=pl.BlockSpec((tm, d), lambda i: (i, 0)),
        out_shape=jax.ShapeDtypeStruct((t, d), F32),
        compiler_params=_params("parallel"),
        name="final_norm",
    )(x, g)


def _split_gate_cols(w_in, bias, n_main):
    n_gate = w_in.shape[1] - n_main
    wg = jnp.zeros((w_in.shape[0], LANES), BF16).at[:, :n_gate].set(w_in[:, n_main:].astype(BF16))
    bg = jnp.zeros((1, LANES), F32).at[0, :n_gate].set(bias)
    return w_in[:, :n_main].astype(BF16), wg, bg


def kernel(x, c, positions, mod_w, mod_b, norm_g, final_g, a_w_in, a_sinks, a_w_out, b_w_in, b_f_bias, b_w_out, c_w_in, c_gate_bias, c_w_out, ffn_w_up, ffn_conv_w, ffn_conv_b, ffn_w_down):
    batch, seq, d = x.shape
    depth = mod_w.shape[0]
    t = batch * seq
    xf = x.reshape(t, d)

    mod = _modulation(c, mod_w, mod_b)[:, :batch, :].reshape(depth, batch, 6, 1, d)
    cos, sins = _rope_tables(positions)

    for i in range(depth):
        sh1, sc1, g1, sh2, sc2, g2 = (mod[i, :, r] for r in range(6))
        gain1 = norm_g[i, 0].reshape(1, d)
        gain2 = norm_g[i, 1].reshape(1, d)
        kind, j = i % N_MIXERS, i // N_MIXERS
        if kind == 0:
            qkv = _proj_a(xf, gain1, sh1, sc1, a_w_in[j].astype(BF16), cos, sins, seq)
            y = _swa(qkv, a_sinks[j], batch, seq)
            w_out = a_w_out[j]
        elif kind == 1:
            heads = b_f_bias.shape[-1]
            n_main = 3 * heads * HEAD_DIM
            w, wg, bg = _split_gate_cols(b_w_in[j], b_f_bias[j], n_main)
            qkv, gl = _proj_gate(xf, gain1, sh1, sc1, w, wg, bg, seq, heads * HEAD_DIM, HEAD_DIM ** -0.5 * LOG2E)
            tk = _pick(seq, (512, 256, 128))
            f, k_aug = _fox_prep(gl, batch, seq, tk)
            y = _fox(qkv, f, k_aug, batch, seq, heads, tk)
            w_out = b_w_out[j]
        else:
            n_main = c_w_in.shape[-1] - 2 * C_HEADS
            dk = n_main // (6 * C_HEADS)
            w, wg, bg = _split_gate_cols(c_w_in[j], c_gate_bias[j], n_main)
            proj, gl = _proj_gate(xf, gain1, sh1, sc1, w, wg, bg, seq, C_HEADS * dk, dk ** -0.5)
            y = _mlstm(proj, gl, batch, seq)
            w_out = c_w_out[j]
        xf = _proj_res(y, w_out.astype(BF16), xf, g1, seq)

        conv_tab = jnp.zeros((SUBLANES, ffn_conv_w.shape[-1]), F32)
        conv_tab = conv_tab.at[:CONV_WIDTH].set(ffn_conv_w[i]).at[CONV_WIDTH].set(ffn_conv_b[i])
        act = _ffn_up(xf, gain2, sh2, sc2, ffn_w_up[i].astype(BF16), conv_tab, seq)
        xf = _proj_res(act, ffn_w_down[i].astype(BF16), xf, g2, seq)

    return _final_norm(xf, final_g.reshape(1, d)).reshape(batch, seq, d)
```

```python
import functools

import jax
import jax.numpy as jnp
from jax import lax
from jax.experimental import pallas as pl
from jax.experimental.pallas import tpu as pltpu

F32 = jnp.float32
BF16 = jnp.bfloat16

EPS = 1e-6
NEG = -1e30
ROPE_THETA = 10000.0
HEAD_DIM = 64
HALF = HEAD_DIM // 2
GQA_GROUP = 8
WINDOW = 128
C_HEADS = 4
N_MIXERS = 3
CONV_WIDTH = 3
LANES = 128
SUBLANES = 8
LOG2E = 1.4426950408889634
VMEM_LIMIT = 56 * 1024 * 1024


def _pick(dim, prefs):
    for p in prefs:
        if dim % p == 0:
            return p
    return dim


def _params(*sem):
    return pltpu.CompilerParams(dimension_semantics=sem, vmem_limit_bytes=VMEM_LIMIT)


def _lane_iota(shape):
    return lax.broadcasted_iota(jnp.int32, shape, len(shape) - 1)


def _log_sigmoid(x):
    return jnp.minimum(x, 0.0) - jnp.log1p(jnp.exp(-jnp.abs(x)))


def _tril_f32(n):
    r = lax.broadcasted_iota(jnp.int32, (n, n), 0)
    c = lax.broadcasted_iota(jnp.int32, (n, n), 1)
    return (c <= r).astype(F32)


def _mod_kernel(c_ref, w_ref, b_ref, o_ref):
    c = c_ref[...]
    act = (c * jax.nn.sigmoid(c)).astype(BF16)
    o_ref[...] = jnp.dot(act, w_ref[...].astype(BF16), preferred_element_type=F32) + b_ref[...]


def _modulation(c, mod_w, mod_b):
    depth, d, n = mod_w.shape
    b = c.shape[0]
    cp = jnp.zeros((SUBLANES, d), F32).at[:b].set(c)
    tn = _pick(n, (1024, 512, 256, 128))
    return pl.pallas_call(
        _mod_kernel,
        grid=(depth, n // tn),
        in_specs=[
            pl.BlockSpec((SUBLANES, d), lambda l, j: (0, 0)),
            pl.BlockSpec((None, d, tn), lambda l, j: (l, 0, j)),
            pl.BlockSpec((None, 1, tn), lambda l, j: (l, 0, j)),
        ],
        out_specs=pl.BlockSpec((None, SUBLANES, tn), lambda l, j: (l, 0, j)),
        out_shape=jax.ShapeDtypeStruct((depth, SUBLANES, n), F32),
        compiler_params=_params("parallel", "parallel"),
        name="modulation",
    )(cp, mod_w, mod_b.reshape(depth, 1, n))


def _rope_table_kernel(pos_ref, freq_ref, cos_ref, sin_ref):
    ang = pos_ref[...] * freq_ref[...]
    lane = _lane_iota(ang.shape)
    cos_ref[...] = jnp.cos(ang)
    s = jnp.sin(ang)
    sin_ref[...] = jnp.where((lane & (HEAD_DIM - 1)) < HALF, -s, s)


def _rope_tables(positions):
    t = positions.size
    pos = positions.reshape(t, 1).astype(F32)
    inv_freq = ROPE_THETA ** (-jnp.arange(0, HEAD_DIM, 2, dtype=F32) / HEAD_DIM)
    freq = jnp.tile(inv_freq, LANES // HALF).reshape(1, LANES)
    ts = _pick(t, (1024, 512, 256, 128))
    return pl.pallas_call(
        _rope_table_kernel,
        grid=(t // ts,),
        in_specs=[pl.BlockSpec((ts, 1), lambda i: (i, 0)), pl.BlockSpec((1, LANES), lambda i: (0, 0))],
        out_specs=[pl.BlockSpec((ts, LANES), lambda i: (i, 0))] * 2,
        out_shape=[jax.ShapeDtypeStruct((t, LANES), F32)] * 2,
        compiler_params=_params("parallel"),
        name="rope_tables",
    )(pos, freq)


def _norm_mod(x_ref, g_ref, sh_ref, sc_ref):
    x = x_ref[...]
    y = x * lax.rsqrt(jnp.mean(x * x, axis=-1, keepdims=True) + EPS)
    y = y * g_ref[...]
    return (y * (1.0 + sc_ref[...]) + sh_ref[...]).astype(BF16)


def _rope(a, cos, sins):
    lane = _lane_iota(a.shape)
    swapped = jnp.where((lane & (HEAD_DIM - 1)) < HALF, pltpu.roll(a, LANES - HALF, 1), pltpu.roll(a, HALF, 1))
    return a * cos + swapped * sins


def _proj_a_kernel(x_ref, g_ref, sh_ref, sc_ref, w_ref, cos_ref, sin_ref, o_ref, *, nq, nkv):
    h = _norm_mod(x_ref, g_ref, sh_ref, sc_ref)
    acc = jnp.dot(h, w_ref[...], preferred_element_type=F32)
    cos = cos_ref[...]
    sins = sin_ref[...]
    for c in range((nq + nkv) // LANES):
        a = _rope(acc[:, c * LANES:(c + 1) * LANES], cos, sins)
        if c * LANES < nq:
            a = a * (HEAD_DIM ** -0.5)
        o_ref[:, c * LANES:(c + 1) * LANES] = a.astype(BF16)
    o_ref[:, nq + nkv:] = acc[:, nq + nkv:].astype(BF16)


def _proj_a(x, g, sh, sc, w, cos, sins, seq):
    t, d = x.shape
    n = w.shape[1]
    nq = d
    nkv = (n - nq) // 2
    tm = _pick(seq, (512, 256, 128))
    per_b = seq // tm
    return pl.pallas_call(
        functools.partial(_proj_a_kernel, nq=nq, nkv=nkv),
        grid=(t // tm,),
        in_specs=[
            pl.BlockSpec((tm, d), lambda i: (i, 0)),
            pl.BlockSpec((1, d), lambda i: (0, 0)),
            pl.BlockSpec((None, 1, d), lambda i: (i // per_b, 0, 0)),
            pl.BlockSpec((None, 1, d), lambda i: (i // per_b, 0, 0)),
            pl.BlockSpec((d, n), lambda i: (0, 0)),
            pl.BlockSpec((tm, LANES), lambda i: (i, 0)),
            pl.BlockSpec((tm, LANES), lambda i: (i, 0)),
        ],
        out_specs=pl.BlockSpec((tm, n), lambda i: (i, 0)),
        out_shape=jax.ShapeDtypeStruct((t, n), BF16),
        compiler_params=_params("parallel"),
        name="proj_swa",
    )(x, g, sh, sc, w, cos, sins)


def _proj_gate_kernel(x_ref, g_ref, sh_ref, sc_ref, w_ref, wg_ref, bg_ref, o_ref, og_ref, h_scr, *,
                      q_tiles, q_scale):
    j = pl.program_id(1)

    @pl.when(j == 0)
    def _():
        h_scr[...] = _norm_mod(x_ref, g_ref, sh_ref, sc_ref)
        og_ref[...] = jnp.dot(h_scr[...], wg_ref[...], preferred_element_type=F32) + bg_ref[...]

    acc = jnp.dot(h_scr[...], w_ref[...], preferred_element_type=F32)
    o_ref[...] = (acc * jnp.where(j < q_tiles, q_scale, 1.0)).astype(BF16)


def _proj_gate(x, g, sh, sc, w, wg, bg, seq, q_cols, q_scale):
    t, d = x.shape
    n = w.shape[1]
    tm = _pick(seq, (1024, 512, 256, 128))
    tn = _pick(q_cols, (1024, 512, 256, 128))
    per_b = seq // tm
    return pl.pallas_call(
        functools.partial(_proj_gate_kernel, q_tiles=q_cols // tn, q_scale=q_scale),
        grid=(t // tm, n // tn),
        in_specs=[
            pl.BlockSpec((tm, d), lambda i, j: (i, 0)),
            pl.BlockSpec((1, d), lambda i, j: (0, 0)),
            pl.BlockSpec((None, 1, d), lambda i, j: (i // per_b, 0, 0)),
            pl.BlockSpec((None, 1, d), lambda i, j: (i // per_b, 0, 0)),
            pl.BlockSpec((d, tn), lambda i, j: (0, j)),
            pl.BlockSpec((d, LANES), lambda i, j: (0, 0)),
            pl.BlockSpec((1, LANES), lambda i, j: (0, 0)),
        ],
        out_specs=[pl.BlockSpec((tm, tn), lambda i, j: (i, j)), pl.BlockSpec((tm, LANES), lambda i, j: (i, 0))],
        out_shape=[jax.ShapeDtypeStruct((t, n), BF16), jax.ShapeDtypeStruct((t, LANES), F32)],
        scratch_shapes=[pltpu.VMEM((tm, d), BF16)],
        compiler_params=_params("parallel", "arbitrary"),
        name="proj_gate",
    )(x, g, sh, sc, w, wg, bg)


def _proj_res_kernel(y_ref, w_ref, x_ref, g_ref, o_ref):
    acc = jnp.dot(y_ref[...], w_ref[...], preferred_element_type=F32)
    o_ref[...] = x_ref[...] + g_ref[...] * acc


def _proj_res(y, w, x, gate, seq):
    t, k = y.shape
    n = w.shape[1]
    tm = _pick(seq, (1024, 512, 256, 128))
    tn = _pick(n, (1024, 512, 256, 128) if k <= n else (512, 256, 128))
    per_b = seq // tm
    return pl.pallas_call(
        _proj_res_kernel,
        grid=(t // tm, n // tn),
        in_specs=[
            pl.BlockSpec((tm, k), lambda i, j: (i, 0)),
            pl.BlockSpec((k, tn), lambda i, j: (0, j)),
            pl.BlockSpec((tm, tn), lambda i, j: (i, j)),
            pl.BlockSpec((None, 1, tn), lambda i, j: (i // per_b, 0, j)),
        ],
        out_specs=pl.BlockSpec((tm, tn), lambda i, j: (i, j)),
        out_shape=jax.ShapeDtypeStruct((t, n), F32),
        input_output_aliases={2: 0},
        compiler_params=_params("parallel", "arbitrary"),
        name="proj_residual",
    )(y, w, x, gate)


def _ffn_up_kernel(x_ref, g_ref, sh_ref, sc_ref, wg_ref, wv_ref, cg_ref, cv_ref, o_ref, h_scr, tg_scr, tv_scr, *,
                   per_b):
    i = pl.program_id(0)
    j = pl.program_id(1)
    tm = o_ref.shape[0]

    @pl.when(j == 0)
    def _():
        h_scr[...] = _norm_mod(x_ref, g_ref, sh_ref, sc_ref)

    @pl.when(i == 0)
    def _():
        tg_scr[j] = jnp.zeros(tg_scr.shape[1:], F32)
        tv_scr[j] = jnp.zeros(tv_scr.shape[1:], F32)

    h = h_scr[...]
    ug = jnp.dot(h, wg_ref[...], preferred_element_type=F32)
    uv = jnp.dot(h, wv_ref[...], preferred_element_type=F32)

    def conv(u, c_ref, shifted1, shifted2):
        return u * c_ref[2:3, :] + shifted1 * c_ref[1:2, :] + shifted2 * c_ref[0:1, :] + c_ref[3:4, :]

    def glu(yg, yv):
        return (yg * jax.nn.sigmoid(yg) * yv).astype(o_ref.dtype)

    yg = conv(ug, cg_ref, pltpu.roll(ug, 1, 0), pltpu.roll(ug, 2, 0))
    yv = conv(uv, cv_ref, pltpu.roll(uv, 1, 0), pltpu.roll(uv, 2, 0))
    o_ref[...] = glu(yg, yv)

    first = (i % per_b) == 0
    row = lax.broadcasted_iota(jnp.int32, (SUBLANES, ug.shape[1]), 0)

    def head_rows(u, t_scr, c_ref):
        top = u[0:SUBLANES, :]
        tail = jnp.where(first, 0.0, t_scr[j])
        s1 = jnp.where(row == 0, pltpu.roll(tail, 1, 0), pltpu.roll(top, 1, 0))
        s2 = jnp.where(row <= 1, pltpu.roll(tail, 2, 0), pltpu.roll(top, 2, 0))
        t_scr[j] = u[tm - SUBLANES:tm, :]
        return conv(top, c_ref, s1, s2)

    o_ref[0:SUBLANES, :] = glu(head_rows(ug, tg_scr, cg_ref), head_rows(uv, tv_scr, cv_ref))


def _ffn_up(x, g, sh, sc, w_up, conv_tab, seq):
    t, d = x.shape
    f = w_up.shape[1] // 2
    tm = _pick(seq, (1024, 512, 256, 128))
    tn = _pick(f, (512, 256, 128))
    nj = f // tn
    per_b = seq // tm
    return pl.pallas_call(
        functools.partial(_ffn_up_kernel, per_b=per_b),
        grid=(t // tm, nj),
        in_specs=[
            pl.BlockSpec((tm, d), lambda i, j: (i, 0)),
            pl.BlockSpec((1, d), lambda i, j: (0, 0)),
            pl.BlockSpec((None, 1, d), lambda i, j: (i // per_b, 0, 0)),
            pl.BlockSpec((None, 1, d), lambda i, j: (i // per_b, 0, 0)),
            pl.BlockSpec((d, tn), lambda i, j: (0, j)),
            pl.BlockSpec((d, tn), lambda i, j: (0, j + nj)),
            pl.BlockSpec((SUBLANES, tn), lambda i, j: (0, j)),
            pl.BlockSpec((SUBLANES, tn), lambda i, j: (0, j + nj)),
        ],
        out_specs=pl.BlockSpec((tm, tn), lambda i, j: (i, j)),
        out_shape=jax.ShapeDtypeStruct((t, f), BF16),
        scratch_shapes=[
            pltpu.VMEM((tm, d), BF16),
            pltpu.VMEM((nj, SUBLANES, tn), F32),
            pltpu.VMEM((nj, SUBLANES, tn), F32),
        ],
        compiler_params=_params("arbitrary", "arbitrary"),
        name="ffn_up_conv_glu",
    )(x, g, sh, sc, w_up, w_up, conv_tab, conv_tab)


def _swa_kernel(q_ref, kp_ref, kc_ref, vp_ref, vc_ref, sink_ref, o_ref, s_scr, p_scr, e_scr, *, heads):
    n = pl.program_id(1)
    blk = q_ref.shape[0]
    kb = jnp.concatenate([kp_ref[...], kc_ref[...]], axis=0).astype(F32)
    vb = jnp.concatenate([vp_ref[...], vc_ref[...]], axis=0).astype(F32)
    lane = _lane_iota((2 * blk, LANES))
    low = lane < HEAD_DIM
    qi = lax.broadcasted_iota(jnp.int32, (blk, 2 * blk), 0)
    kj = lax.broadcasted_iota(jnp.int32, (blk, 2 * blk), 1)
    rel = qi + blk - kj
    first_valid_key = jnp.where(n > 0, 0, blk)
    mask = (rel >= 0) & (rel < WINDOW) & (kj >= first_valid_key)
    qlow = _lane_iota((blk, LANES)) < HEAD_DIM
    contract_lanes = (((1,), (1,)), ((), ()))

    kv_cache = {}

    def kv_group(g):
        if g not in kv_cache:
            c, half = divmod(g, 2)
            kc = kb[:, c * LANES:(c + 1) * LANES]
            vc = vb[:, c * LANES:(c + 1) * LANES]
            kr = pltpu.roll(kc, HEAD_DIM, 1)
            vr = pltpu.roll(vc, HEAD_DIM, 1)
            k2 = (jnp.where(low, kc, kr) if half == 0 else jnp.where(low, kr, kc)).astype(BF16)
            v_lo, v_hi = (vc, vr) if half == 0 else (vr, vc)
            v_even = jnp.where(low, v_lo, 1.0).astype(BF16)
            v_odd = jnp.where(low, 1.0, v_hi).astype(BF16)
            kv_cache[g] = (k2, v_even, v_odd)
        return kv_cache[g]

    for p in range(heads // 2):
        k2 = kv_group((2 * p) // GQA_GROUP)[0]
        qp = q_ref[:, p * LANES:(p + 1) * LANES]
        for par in range(2):
            qh = jnp.where(qlow if par == 0 else ~qlow, qp, jnp.zeros_like(qp))
            s_scr[2 * p + par] = lax.dot_general(qh, k2, contract_lanes, preferred_element_type=F32)

    s = jnp.where(mask[None], s_scr[...], NEG)
    sink = sink_ref[...]
    m = jnp.broadcast_to(jnp.maximum(jnp.max(s, axis=-1, keepdims=True), sink), e_scr.shape)
    p_scr[...] = jnp.exp((s - jnp.concatenate([m] * (2 * blk // LANES), axis=-1)).astype(BF16))
    e_scr[...] = jnp.exp(sink - m)

    for p in range(heads // 2):
        _, v_even, v_odd = kv_group((2 * p) // GQA_GROUP)
        outs = []
        for par in range(2):
            hq = 2 * p + par
            acc = jnp.dot(p_scr[hq], v_even if par == 0 else v_odd, preferred_element_type=F32)
            outs.append(acc / (pltpu.roll(acc, HEAD_DIM, 1) + e_scr[hq]))
        o_ref[:, p * LANES:(p + 1) * LANES] = jnp.where(qlow, outs[0], outs[1]).astype(o_ref.dtype)


def _swa(qkv, sinks, batch, seq):
    t, n = qkv.shape
    heads = sinks.shape[-1]
    nq = heads * HEAD_DIM
    nkv = (n - nq) // 2
    blk = WINDOW
    nb = seq // blk
    kcol = nq // nkv
    return pl.pallas_call(
        functools.partial(_swa_kernel, heads=heads),
        grid=(batch, nb),
        in_specs=[
            pl.BlockSpec((blk, nq), lambda b, i: (b * nb + i, 0)),
            pl.BlockSpec((blk, nkv), lambda b, i: (b * nb + jnp.maximum(i - 1, 0), kcol)),
            pl.BlockSpec((blk, nkv), lambda b, i: (b * nb + i, kcol)),
            pl.BlockSpec((blk, nkv), lambda b, i: (b * nb + jnp.maximum(i - 1, 0), kcol + 1)),
            pl.BlockSpec((blk, nkv), lambda b, i: (b * nb + i, kcol + 1)),
            pl.BlockSpec((heads, 1, 1), lambda b, i: (0, 0, 0)),
        ],
        out_specs=pl.BlockSpec((blk, nq), lambda b, i: (b * nb + i, 0)),
        out_shape=jax.ShapeDtypeStruct((t, nq), BF16),
        scratch_shapes=[
            pltpu.VMEM((heads, blk, 2 * blk), F32),
            pltpu.VMEM((heads, blk, 2 * blk), BF16),
            pltpu.VMEM((heads, blk, LANES), F32),
        ],
        compiler_params=_params("parallel", "parallel"),
        name="swa_attention",
    )(qkv, qkv, qkv, qkv, qkv, sinks.reshape(heads, 1, 1))


AUG_HEADS = LANES // 4
AUG_Q0 = 3 * AUG_HEADS


def _split3(x):
    hi = x.astype(BF16).astype(F32)
    r1 = x - hi
    mid = r1.astype(BF16).astype(F32)
    lo = (r1 - mid).astype(BF16).astype(F32)
    return hi, mid, lo


def _fox_prep_kernel(gl_ref, f_ref, ka_ref, carry_ref):
    @pl.when(pl.program_id(1) == 0)
    def _():
        carry_ref[...] = jnp.zeros_like(carry_ref)

    ts = gl_ref.shape[0]
    logf = _log_sigmoid(gl_ref[...]) * LOG2E
    f = jnp.dot(_tril_f32(ts), logf, preferred_element_type=F32, precision=lax.Precision.HIGHEST) + carry_ref[...]
    f_ref[...] = f
    carry_ref[...] = f[ts - 1:ts, :]

    r = lax.broadcasted_iota(jnp.int32, (LANES, LANES), 0)
    c = lax.broadcasted_iota(jnp.int32, (LANES, LANES), 1)
    aug = jnp.where((_lane_iota(f.shape) >= AUG_Q0) & (_lane_iota(f.shape) < AUG_Q0 + 6), 1.0, 0.0)
    for p, part in enumerate(_split3(-f)):
        place = ((c == 3 * r + p) & (r < AUG_HEADS)).astype(BF16)
        aug = aug + jnp.dot(part.astype(BF16), place, preferred_element_type=F32)
    ka_ref[...] = aug.astype(BF16)


def _fox_prep(gate_logits, batch, seq, ts):
    ns = seq // ts
    return pl.pallas_call(
        _fox_prep_kernel,
        grid=(batch, ns),
        in_specs=[pl.BlockSpec((ts, LANES), lambda b, s: (b * ns + s, 0))],
        out_specs=[pl.BlockSpec((ts, LANES), lambda b, s: (b * ns + s, 0))] * 2,
        out_shape=[
            jax.ShapeDtypeStruct((batch * seq, LANES), F32),
            jax.ShapeDtypeStruct((batch * seq, LANES), BF16),
        ],
        scratch_shapes=[pltpu.VMEM((1, LANES), F32)],
        compiler_params=_params("parallel", "arbitrary"),
        name="fox_gate_cumsum",
    )(gate_logits)


def _fox_kernel(q_ref, k_ref, v_ref, f_ref, ka_ref, o_ref, m_scr, alpha_scr, acc_scr, s_scr, p_scr, ve_scr, vo_scr,
                *, tq, tk):
    hp = pl.program_id(1)
    i = pl.program_id(2)

    @pl.when(i == 0)
    def _():
        v = v_ref[...]
        vlow = _lane_iota(v.shape) < HEAD_DIM
        ones = jnp.ones_like(v)
        ve_scr[...] = jnp.where(vlow, v, ones)
        vo_scr[...] = jnp.where(vlow, ones, v)

    lane = _lane_iota((tq, LANES))
    low = lane < HEAD_DIM
    q = q_ref[...]
    f_tile = f_ref[...]
    contract_lanes = (((1,), (1,)), ((), ()))

    q_aug = []
    for par in range(2):
        head = 2 * hp + par
        fq = jnp.sum(jnp.where(lane == head, f_tile, 0.0), axis=-1, keepdims=True)
        hi, mid, lo = _split3(fq)
        a = jnp.where((lane >= 3 * head) & (lane < 3 * head + 3), 1.0, 0.0)
        a = jnp.where(lane == AUG_Q0 + 3 * par, hi, a)
        a = jnp.where(lane == AUG_Q0 + 3 * par + 1, mid, a)
        a = jnp.where(lane == AUG_Q0 + 3 * par + 2, lo, a)
        qh = jnp.where(low, q, jnp.zeros_like(q)) if par == 0 else jnp.where(low, jnp.zeros_like(q), q)
        q_aug.append(jnp.concatenate([qh, a.astype(BF16)], axis=1))

    m_scr[...] = jnp.full(m_scr.shape, NEG, F32)
    acc_scr[...] = jnp.zeros(acc_scr.shape, F32)
    n_full = (i * tq) // tk

    def tile(j, masked, r0=0):
        start = pl.multiple_of(j * tk, tk)
        k_aug = jnp.concatenate([k_ref[pl.ds(start, tk), :], ka_ref[pl.ds(start, tk), :]], axis=1)
        v_heads = (ve_scr[pl.ds(start, tk), :], vo_scr[pl.ds(start, tk), :])
        for par in range(2):
            s_scr[par, r0:, :] = lax.dot_general(q_aug[par][r0:, :], k_aug, contract_lanes,
                                                 preferred_element_type=F32)
        for par in range(2):
            z = s_scr[par, r0:, :]
            if masked:
                r = lax.broadcasted_iota(jnp.int32, z.shape, 0) + (i * tq + r0)
                c = lax.broadcasted_iota(jnp.int32, z.shape, 1) + start
                z = jnp.where(c <= r, z, NEG)
            m_prev = m_scr[par, r0:, :]
            m_new = jnp.maximum(m_prev, jnp.max(z, axis=-1, keepdims=True))
            alpha_scr[par, r0:, :] = jnp.exp2(m_prev - m_new)
            m_scr[par, r0:, :] = m_new
            p_scr[par, r0:, :] = jnp.exp2((z - jnp.concatenate([m_new] * (tk // LANES), axis=1)).astype(BF16))
        for par in range(2):
            acc_scr[par, r0:, :] = alpha_scr[par, r0:, :] * acc_scr[par, r0:, :] + jnp.dot(
                p_scr[par, r0:, :], v_heads[par], preferred_element_type=F32)

    def body(j, carry):
        tile(j, False)
        return carry

    lax.fori_loop(0, n_full, body, 0)
    for d in range(max(tq // tk, 1)):
        tile(n_full + d, True, r0=d * tk)

    a0 = acc_scr[0]
    a1 = acc_scr[1]
    o_ref[...] = jnp.where(low, a0 / pltpu.roll(a0, HEAD_DIM, 1), a1 / pltpu.roll(a1, HEAD_DIM, 1)).astype(o_ref.dtype)


def _fox(qkv, f, k_aug, batch, seq, heads, tk):
    assert heads <= AUG_HEADS
    t = qkv.shape[0]
    tq = _pick(seq, (1024, 512, 256, 128))
    nq = seq // tq
    pairs = heads // 2
    return pl.pallas_call(
        functools.partial(_fox_kernel, tq=tq, tk=tk),
        grid=(batch, pairs, nq),
        in_specs=[
            pl.BlockSpec((tq, LANES), lambda b, p, i: (b * nq + i, p)),
            pl.BlockSpec((seq, LANES), lambda b, p, i: (b, pairs + p)),
            pl.BlockSpec((seq, LANES), lambda b, p, i: (b, 2 * pairs + p)),
            pl.BlockSpec((tq, LANES), lambda b, p, i: (b * nq + i, 0)),
            pl.BlockSpec((seq, LANES), lambda b, p, i: (b, 0)),
        ],
        out_specs=pl.BlockSpec((tq, LANES), lambda b, p, i: (b * nq + i, p)),
        out_shape=jax.ShapeDtypeStruct((t, heads * HEAD_DIM), BF16),
        scratch_shapes=[
            pltpu.VMEM((2, tq, LANES), F32),
            pltpu.VMEM((2, tq, LANES), F32),
            pltpu.VMEM((2, tq, LANES), F32),
            pltpu.VMEM((2, tq, tk), F32),
            pltpu.VMEM((2, tq, tk), BF16),
            pltpu.VMEM((seq, LANES), BF16),
            pltpu.VMEM((seq, LANES), BF16),
        ],
        compiler_params=_params("parallel", "parallel", "arbitrary"),
        name="fox_attention",
    )(qkv, qkv, qkv, f, k_aug)


def _mlstm_kernel(q_ref, k_ref, v_ref, o_ref, g_ref, out_ref, c_scr, m_scr, *, dk, dv):
    @pl.when(pl.program_id(1) == 0)
    def _():
        c_scr[...] = jnp.zeros_like(c_scr)
        m_scr[...] = jnp.zeros_like(m_scr)

    L = g_ref.shape[0]
    gates = g_ref[...]
    b_all = jnp.dot(_tril_f32(L), _log_sigmoid(gates), preferred_element_type=F32,
                    precision=lax.Precision.HIGHEST)
    gates_t = gates.T
    b_all_t = b_all.T
    r = lax.broadcasted_iota(jnp.int32, (L, L), 0)
    c = lax.broadcasted_iota(jnp.int32, (L, L), 1)
    tri = c <= r
    ones_col = (_lane_iota((L, LANES)) == 0).astype(BF16)
    contract_lanes = (((1,), (1,)), ((), ()))
    contract_rows = (((0,), (0,)), ((), ()))

    for hd in range(C_HEADS):
        q = q_ref[:, hd * dk:(hd + 1) * dk]
        k = k_ref[:, hd * dk:(hd + 1) * dk]
        v_aug = jnp.concatenate([v_ref[:, hd * dv:(hd + 1) * dv], ones_col], axis=1)
        b_col = b_all[:, C_HEADS + hd:C_HEADS + hd + 1]
        i_col = gates[:, hd:hd + 1]
        b_row = b_all_t[C_HEADS + hd:C_HEADS + hd + 1, :]
        i_row = gates_t[hd:hd + 1, :]
        m_prev = m_scr[hd:hd + 1, 0:1]
        g_last = b_col[L - 1:L, :]

        dm = jnp.where(tri, b_col - b_row + i_row, NEG)
        inter = b_col + m_prev
        m_t = jnp.maximum(inter, jnp.max(dm, axis=-1, keepdims=True))
        a = jnp.exp(dm - m_t) * lax.dot_general(q, k, contract_lanes, preferred_element_type=F32)
        sc = jnp.exp(inter - m_t)
        state = c_scr[hd]
        num = sc * jnp.dot(q, state.astype(BF16), preferred_element_type=F32) \
            + jnp.dot(a.astype(BF16), v_aug, preferred_element_type=F32)
        den = num[:, dv:dv + 1]
        h_t = num[:, :dv] / jnp.maximum(jnp.abs(den), jnp.exp(-m_t))
        gate_o = jax.nn.sigmoid(o_ref[:, hd * dv:(hd + 1) * dv].astype(F32))
        out_ref[:, hd * dv:(hd + 1) * dv] = (gate_o * h_t).astype(out_ref.dtype)

        wlog = g_last - b_col + i_col
        m_new = jnp.maximum(g_last + m_prev, jnp.max(wlog, axis=0, keepdims=True))
        decay = jnp.exp(g_last + m_prev - m_new)
        kw = (k.astype(F32) * jnp.exp(wlog - m_new)).astype(BF16)
        c_scr[hd] = decay * state + lax.dot_general(kw, v_aug, contract_rows, preferred_element_type=F32)
        m_scr[hd:hd + 1, :] = jnp.broadcast_to(m_new, (1, LANES))


def _mlstm(proj, gates, batch, seq):
    t, n = proj.shape
    dv = n // (3 * C_HEADS)
    dk = dv // 2
    L = _pick(seq, (256, 128, 64))
    nc = seq // L
    hq = C_HEADS * dk
    hv = C_HEADS * dv
    qb, vb = hq // hq, (2 * hq) // hv
    return pl.pallas_call(
        functools.partial(_mlstm_kernel, dk=dk, dv=dv),
        grid=(batch, nc),
        in_specs=[
            pl.BlockSpec((L, hq), lambda b, s: (b * nc + s, 0)),
            pl.BlockSpec((L, hq), lambda b, s: (b * nc + s, qb)),
            pl.BlockSpec((L, hv), lambda b, s: (b * nc + s, vb)),
            pl.BlockSpec((L, hv), lambda b, s: (b * nc + s, vb + 1)),
            pl.BlockSpec((L, LANES), lambda b, s: (b * nc + s, 0)),
        ],
        out_specs=pl.BlockSpec((L, hv), lambda b, s: (b * nc + s, 0)),
        out_shape=jax.ShapeDtypeStruct((t, hv), BF16),
        scratch_shapes=[pltpu.VMEM((C_HEADS, dk, dv + LANES), F32), pltpu.VMEM((SUBLANES, LANES), F32)],
        compiler_params=_params("parallel", "arbitrary"),
        name="mlstm",
    )(proj, proj, proj, proj, gates)


def _final_norm_kernel(x_ref, g_ref, o_ref):
    x = x_ref[...]
    o_ref[...] = x * lax.rsqrt(jnp.mean(x * x, axis=-1, keepdims=True) + EPS) * g_ref[...]


def _final_norm(x, g):
    t, d = x.shape
    tm = _pick(t, (1024, 512, 256, 128))
    return pl.pallas_call(
        _final_norm_kernel,
        grid=(t // tm,),
        in_specs=[pl.BlockSpec((tm, d), lambda i: (i, 0)), pl.BlockSpec((1, d), lambda i: (0, 0))],
        out_specs=pl.BlockSpec((tm, d), lambda i: (i, 0)),
        out_shape=jax.ShapeDtypeStruct((t, d), F32),
        compiler_params=_params("parallel"),
        name="final_norm",
    )(x, g)


def _split_gate_cols(w_in, bias, n_main):
    n_gate = w_in.shape[1] - n_main
    wg = jnp.zeros((w_in.shape[0], LANES), BF16).at[:, :n_gate].set(w_in[:, n_main:].astype(BF16))
    bg = jnp.zeros((1, LANES), F32).at[0, :n_gate].set(bias)
    return w_in[:, :n_main].astype(BF16), wg, bg


def kernel(x, c, positions, mod_w, mod_b, norm_g, final_g, a_w_in, a_sinks, a_w_out, b_w_in, b_f_bias, b_w_out, c_w_in, c_gate_bias, c_w_out, ffn_w_up, ffn_conv_w, ffn_conv_b, ffn_w_down):
    batch, seq, d = x.shape
    depth = mod_w.shape[0]
    t = batch * seq
    xf = x.reshape(t, d)

    mod = _modulation(c, mod_w, mod_b)[:, :batch, :].reshape(depth, batch, 6, 1, d)
    cos, sins = _rope_tables(positions)

    for i in range(depth):
        sh1, sc1, g1, sh2, sc2, g2 = (mod[i, :, r] for r in range(6))
        gain1 = norm_g[i, 0].reshape(1, d)
        gain2 = norm_g[i, 1].reshape(1, d)
        kind, j = i % N_MIXERS, i // N_MIXERS
        if kind == 0:
            qkv = _proj_a(xf, gain1, sh1, sc1, a_w_in[j].astype(BF16), cos, sins, seq)
            y = _swa(qkv, a_sinks[j], batch, seq)
            w_out = a_w_out[j]
        elif kind == 1:
            heads = b_f_bias.shape[-1]
            n_main = 3 * heads * HEAD_DIM
            w, wg, bg = _split_gate_cols(b_w_in[j], b_f_bias[j], n_main)
            qkv, gl = _proj_gate(xf, gain1, sh1, sc1, w, wg, bg, seq, heads * HEAD_DIM, HEAD_DIM ** -0.5 * LOG2E)
            tk = _pick(seq, (512, 256, 128))
            f, k_aug = _fox_prep(gl, batch, seq, tk)
            y = _fox(qkv, f, k_aug, batch, seq, heads, tk)
            w_out = b_w_out[j]
        else:
            n_main = c_w_in.shape[-1] - 2 * C_HEADS
            dk = n_main // (6 * C_HEADS)
            w, wg, bg = _split_gate_cols(c_w_in[j], c_gate_bias[j], n_main)
            proj, gl = _proj_gate(xf, gain1, sh1, sc1, w, wg, bg, seq, C_HEADS * dk, dk ** -0.5)
            y = _mlstm(proj, gl, batch, seq)
            w_out = c_w_out[j]
        xf = _proj_res(y, w_out.astype(BF16), xf, g1, seq)

        conv_tab = jnp.zeros((SUBLANES, ffn_conv_w.shape[-1]), F32)
        conv_tab = conv_tab.at[:CONV_WIDTH].set(ffn_conv_w[i]).at[CONV_WIDTH].set(ffn_conv_b[i])
        act = _ffn_up(xf, gain2, sh2, sc2, ffn_w_up[i].astype(BF16), conv_tab, seq)
        xf = _proj_res(act, ffn_w_down[i].astype(BF16), xf, g2, seq)

    return _final_norm(xf, final_g.reshape(1, d)).reshape(batch, seq, d)
```

```python
import functools

import jax
import jax.numpy as jnp
from jax import lax
from jax.experimental import pallas as pl
from jax.experimental.pallas import tpu as pltpu

F32 = jnp.float32
BF16 = jnp.bfloat16

EPS = 1e-6
NEG = -1e30
ROPE_THETA = 10000.0
HEAD_DIM = 64
HALF = HEAD_DIM // 2
GQA_GROUP = 8
WINDOW = 128
C_HEADS = 4
N_MIXERS = 3
CONV_WIDTH = 3
LANES = 128
SUBLANES = 8
LOG2E = 1.4426950408889634
VMEM_LIMIT = 56 * 1024 * 1024


def _pick(dim, prefs):
    for p in prefs:
        if dim % p == 0:
            return p
    return dim


def _params(*sem):
    return pltpu.CompilerParams(dimension_semantics=sem, vmem_limit_bytes=VMEM_LIMIT)


def _lane_iota(shape):
    return lax.broadcasted_iota(jnp.int32, shape, len(shape) - 1)


def _log_sigmoid(x):
    return jnp.minimum(x, 0.0) - jnp.log1p(jnp.exp(-jnp.abs(x)))


def _tril_f32(n):
    r = lax.broadcasted_iota(jnp.int32, (n, n), 0)
    c = lax.broadcasted_iota(jnp.int32, (n, n), 1)
    return (c <= r).astype(F32)


def _mod_kernel(c_ref, w_ref, b_ref, o_ref):
    c = c_ref[...]
    act = (c * jax.nn.sigmoid(c)).astype(BF16)
    o_ref[...] = jnp.dot(act, w_ref[...].astype(BF16), preferred_element_type=F32) + b_ref[...]


def _modulation(c, mod_w, mod_b):
    depth, d, n = mod_w.shape
    b = c.shape[0]
    cp = jnp.zeros((SUBLANES, d), F32).at[:b].set(c)
    tn = _pick(n, (1024, 512, 256, 128))
    return pl.pallas_call(
        _mod_kernel,
        grid=(depth, n // tn),
        in_specs=[
            pl.BlockSpec((SUBLANES, d), lambda l, j: (0, 0)),
            pl.BlockSpec((None, d, tn), lambda l, j: (l, 0, j)),
            pl.BlockSpec((None, 1, tn), lambda l, j: (l, 0, j)),
        ],
        out_specs=pl.BlockSpec((None, SUBLANES, tn), lambda l, j: (l, 0, j)),
        out_shape=jax.ShapeDtypeStruct((depth, SUBLANES, n), F32),
        compiler_params=_params("parallel", "parallel"),
        name="modulation",
    )(cp, mod_w, mod_b.reshape(depth, 1, n))


def _rope_table_kernel(pos_ref, freq_ref, cos_ref, sin_ref):
    ang = pos_ref[...] * freq_ref[...]
    lane = _lane_iota(ang.shape)
    cos_ref[...] = jnp.cos(ang)
    s = jnp.sin(ang)
    sin_ref[...] = jnp.where((lane & (HEAD_DIM - 1)) < HALF, -s, s)


def _rope_tables(positions):
    t = positions.size
    pos = positions.reshape(t, 1).astype(F32)
    inv_freq = ROPE_THETA ** (-jnp.arange(0, HEAD_DIM, 2, dtype=F32) / HEAD_DIM)
    freq = jnp.tile(inv_freq, LANES // HALF).reshape(1, LANES)
    ts = _pick(t, (1024, 512, 256, 128))
    return pl.pallas_call(
        _rope_table_kernel,
        grid=(t // ts,),
        in_specs=[pl.BlockSpec((ts, 1), lambda i: (i, 0)), pl.BlockSpec((1, LANES), lambda i: (0, 0))],
        out_specs=[pl.BlockSpec((ts, LANES), lambda i: (i, 0))] * 2,
        out_shape=[jax.ShapeDtypeStruct((t, LANES), F32)] * 2,
        compiler_params=_params("parallel"),
        name="rope_tables",
    )(pos, freq)


def _norm_mod(x_ref, g_ref, sh_ref, sc_ref):
    x = x_ref[...]
    y = x * lax.rsqrt(jnp.mean(x * x, axis=-1, keepdims=True) + EPS)
    y = y * g_ref[...]
    return (y * (1.0 + sc_ref[...]) + sh_ref[...]).astype(BF16)


def _rope(a, cos, sins):
    lane = _lane_iota(a.shape)
    swapped = jnp.where((lane & (HEAD_DIM - 1)) < HALF, pltpu.roll(a, LANES - HALF, 1), pltpu.roll(a, HALF, 1))
    return a * cos + swapped * sins


def _proj_a_kernel(x_ref, g_ref, sh_ref, sc_ref, w_ref, cos_ref, sin_ref, o_ref, *, nq, nkv):
    h = _norm_mod(x_ref, g_ref, sh_ref, sc_ref)
    acc = jnp.dot(h, w_ref[...], preferred_element_type=F32)
    cos = cos_ref[...]
    sins = sin_ref[...]
    for c in range((nq + nkv) // LANES):
        a = _rope(acc[:, c * LANES:(c + 1) * LANES], cos, sins)
        if c * LANES < nq:
            a = a * (HEAD_DIM ** -0.5)
        o_ref[:, c * LANES:(c + 1) * LANES] = a.astype(BF16)
    o_ref[:, nq + nkv:] = acc[:, nq + nkv:].astype(BF16)


def _proj_a(x, g, sh, sc, w, cos, sins, seq):
    t, d = x.shape
    n = w.shape[1]
    nq = d
    nkv = (n - nq) // 2
    tm = _pick(seq, (512, 256, 128))
    per_b = seq // tm
    return pl.pallas_call(
        functools.partial(_proj_a_kernel, nq=nq, nkv=nkv),
        grid=(t // tm,),
        in_specs=[
            pl.BlockSpec((tm, d), lambda i: (i, 0)),
            pl.BlockSpec((1, d), lambda i: (0, 0)),
            pl.BlockSpec((None, 1, d), lambda i: (i // per_b, 0, 0)),
            pl.BlockSpec((None, 1, d), lambda i: (i // per_b, 0, 0)),
            pl.BlockSpec((d, n), lambda i: (0, 0)),
            pl.BlockSpec((tm, LANES), lambda i: (i, 0)),
            pl.BlockSpec((tm, LANES), lambda i: (i, 0)),
        ],
        out_specs=pl.BlockSpec((tm, n), lambda i: (i, 0)),
        out_shape=jax.ShapeDtypeStruct((t, n), BF16),
        compiler_params=_params("parallel"),
        name="proj_swa",
    )(x, g, sh, sc, w, cos, sins)


def _proj_gate_kernel(x_ref, g_ref, sh_ref, sc_ref, w_ref, wg_ref, bg_ref, o_ref, og_ref, h_scr, *,
                      q_tiles, q_scale):
    j = pl.program_id(1)

    @pl.when(j == 0)
    def _():
        h_scr[...] = _norm_mod(x_ref, g_ref, sh_ref, sc_ref)
        og_ref[...] = jnp.dot(h_scr[...], wg_ref[...], preferred_element_type=F32) + bg_ref[...]

    acc = jnp.dot(h_scr[...], w_ref[...], preferred_element_type=F32)
    o_ref[...] = (acc * jnp.where(j < q_tiles, q_scale, 1.0)).astype(BF16)


def _proj_gate(x, g, sh, sc, w, wg, bg, seq, q_cols, q_scale):
    t, d = x.shape
    n = w.shape[1]
    tm = _pick(seq, (1024, 512, 256, 128))
    tn = _pick(q_cols, (1024, 512, 256, 128))
    per_b = seq // tm
    return pl.pallas_call(
        functools.partial(_proj_gate_kernel, q_tiles=q_cols // tn, q_scale=q_scale),
        grid=(t // tm, n // tn),
        in_specs=[
            pl.BlockSpec((tm, d), lambda i, j: (i, 0)),
            pl.BlockSpec((1, d), lambda i, j: (0, 0)),
            pl.BlockSpec((None, 1, d), lambda i, j: (i // per_b, 0, 0)),
            pl.BlockSpec((None, 1, d), lambda i, j: (i // per_b, 0, 0)),
            pl.BlockSpec((d, tn), lambda i, j: (0, j)),
            pl.BlockSpec((d, LANES), lambda i, j: (0, 0)),
            pl.BlockSpec((1, LANES), lambda i, j: (0, 0)),
        ],
        out_specs=[pl.BlockSpec((tm, tn), lambda i, j: (i, j)), pl.BlockSpec((tm, LANES), lambda i, j: (i, 0))],
        out_shape=[jax.ShapeDtypeStruct((t, n), BF16), jax.ShapeDtypeStruct((t, LANES), F32)],
        scratch_shapes=[pltpu.VMEM((tm, d), BF16)],
        compiler_params=_params("parallel", "arbitrary"),
        name="proj_gate",
    )(x, g, sh, sc, w, wg, bg)


def _proj_res_kernel(y_ref, w_ref, x_ref, g_ref, o_ref):
    acc = jnp.dot(y_ref[...], w_ref[...], preferred_element_type=F32)
    o_ref[...] = x_ref[...] + g_ref[...] * acc


def _proj_res(y, w, x, gate, seq):
    t, k = y.shape
    n = w.shape[1]
    tm = _pick(seq, (1024, 512, 256, 128))
    tn = _pick(n, (1024, 512, 256, 128) if k <= n else (512, 256, 128))
    per_b = seq // tm
    return pl.pallas_call(
        _proj_res_kernel,
        grid=(t // tm, n // tn),
        in_specs=[
            pl.BlockSpec((tm, k), lambda i, j: (i, 0)),
            pl.BlockSpec((k, tn), lambda i, j: (0, j)),
            pl.BlockSpec((tm, tn), lambda i, j: (i, j)),
            pl.BlockSpec((None, 1, tn), lambda i, j: (i // per_b, 0, j)),
        ],
        out_specs=pl.BlockSpec((tm, tn), lambda i, j: (i, j)),
        out_shape=jax.ShapeDtypeStruct((t, n), F32),
        input_output_aliases={2: 0},
        compiler_params=_params("parallel", "arbitrary"),
        name="proj_residual",
    )(y, w, x, gate)


def _proj_res_norm_kernel(y_ref, w_ref, x_ref, g_ref, fg_ref, o_ref, r_scr):
    j = pl.program_id(1)
    nj, _, tn = r_scr.shape
    acc = jnp.dot(y_ref[...], w_ref[...], preferred_element_type=F32)
    r_scr[j] = x_ref[...] + g_ref[...] * acc

    @pl.when(j == nj - 1)
    def _():
        ssq = sum(jnp.sum(r_scr[c] * r_scr[c], axis=-1, keepdims=True) for c in range(nj))
        inv = lax.rsqrt(ssq / (nj * tn) + EPS)
        for c in range(nj):
            o_ref[:, c * tn:(c + 1) * tn] = r_scr[c] * inv * fg_ref[:, c * tn:(c + 1) * tn]


def _proj_res_norm(y, w, x, gate, final_g, seq):
    t, k = y.shape
    n = w.shape[1]
    tm = _pick(seq, (512, 256, 128))
    tn = _pick(n, (512, 256, 128))
    per_b = seq // tm
    return pl.pallas_call(
        _proj_res_norm_kernel,
        grid=(t // tm, n // tn),
        in_specs=[
            pl.BlockSpec((tm, k), lambda i, j: (i, 0)),
            pl.BlockSpec((k, tn), lambda i, j: (0, j)),
            pl.BlockSpec((tm, tn), lambda i, j: (i, j)),
            pl.BlockSpec((None, 1, tn), lambda i, j: (i // per_b, 0, j)),
            pl.BlockSpec((1, n), lambda i, j: (0, 0)),
        ],
        out_specs=pl.BlockSpec((tm, n), lambda i, j: (i, 0)),
        out_shape=jax.ShapeDtypeStruct((t, n), F32),
        scratch_shapes=[pltpu.VMEM((n // tn, tm, tn), F32)],
        compiler_params=_params("parallel", "arbitrary"),
        name="proj_residual_final_norm",
    )(y, w, x, gate, final_g)


def _ffn_up_kernel(x_ref, g_ref, sh_ref, sc_ref, wg_ref, wv_ref, cg_ref, cv_ref, o_ref, h_scr, tg_scr, tv_scr, *,
                   per_b):
    i = pl.program_id(0)
    j = pl.program_id(1)
    tm = o_ref.shape[0]

    @pl.when(j == 0)
    def _():
        h_scr[...] = _norm_mod(x_ref, g_ref, sh_ref, sc_ref)

    @pl.when(i == 0)
    def _():
        tg_scr[j] = jnp.zeros(tg_scr.shape[1:], F32)
        tv_scr[j] = jnp.zeros(tv_scr.shape[1:], F32)

    h = h_scr[...]
    ug = jnp.dot(h, wg_ref[...], preferred_element_type=F32)
    uv = jnp.dot(h, wv_ref[...], preferred_element_type=F32)

    def conv(u, c_ref, shifted1, shifted2):
        return u * c_ref[2:3, :] + shifted1 * c_ref[1:2, :] + shifted2 * c_ref[0:1, :] + c_ref[3:4, :]

    def glu(yg, yv):
        return (yg * jax.nn.sigmoid(yg) * yv).astype(o_ref.dtype)

    yg = conv(ug, cg_ref, pltpu.roll(ug, 1, 0), pltpu.roll(ug, 2, 0))
    yv = conv(uv, cv_ref, pltpu.roll(uv, 1, 0), pltpu.roll(uv, 2, 0))
    o_ref[...] = glu(yg, yv)

    first = (i % per_b) == 0
    row = lax.broadcasted_iota(jnp.int32, (SUBLANES, ug.shape[1]), 0)

    def head_rows(u, t_scr, c_ref):
        top = u[0:SUBLANES, :]
        tail = jnp.where(first, 0.0, t_scr[j])
        s1 = jnp.where(row == 0, pltpu.roll(tail, 1, 0), pltpu.roll(top, 1, 0))
        s2 = jnp.where(row <= 1, pltpu.roll(tail, 2, 0), pltpu.roll(top, 2, 0))
        t_scr[j] = u[tm - SUBLANES:tm, :]
        return conv(top, c_ref, s1, s2)

    o_ref[0:SUBLANES, :] = glu(head_rows(ug, tg_scr, cg_ref), head_rows(uv, tv_scr, cv_ref))


def _ffn_up(x, g, sh, sc, w_up, conv_tab, seq):
    t, d = x.shape
    f = w_up.shape[1] // 2
    tm = _pick(seq, (1024, 512, 256, 128))
    tn = _pick(f, (512, 256, 128))
    nj = f // tn
    per_b = seq // tm
    return pl.pallas_call(
        functools.partial(_ffn_up_kernel, per_b=per_b),
        grid=(t // tm, nj),
        in_specs=[
            pl.BlockSpec((tm, d), lambda i, j: (i, 0)),
            pl.BlockSpec((1, d), lambda i, j: (0, 0)),
            pl.BlockSpec((None, 1, d), lambda i, j: (i // per_b, 0, 0)),
            pl.BlockSpec((None, 1, d), lambda i, j: (i // per_b, 0, 0)),
            pl.BlockSpec((d, tn), lambda i, j: (0, j)),
            pl.BlockSpec((d, tn), lambda i, j: (0, j + nj)),
            pl.BlockSpec((SUBLANES, tn), lambda i, j: (0, j)),
            pl.BlockSpec((SUBLANES, tn), lambda i, j: (0, j + nj)),
        ],
        out_specs=pl.BlockSpec((tm, tn), lambda i, j: (i, j)),
        out_shape=jax.ShapeDtypeStruct((t, f), BF16),
        scratch_shapes=[
            pltpu.VMEM((tm, d), BF16),
            pltpu.VMEM((nj, SUBLANES, tn), F32),
            pltpu.VMEM((nj, SUBLANES, tn), F32),
        ],
        compiler_params=_params("arbitrary", "arbitrary"),
        name="ffn_up_conv_glu",
    )(x, g, sh, sc, w_up, w_up, conv_tab, conv_tab)


def _swa_kernel(q_ref, kp_ref, kc_ref, vp_ref, vc_ref, sink_ref, o_ref, s_scr, p_scr, e_scr, *, heads):
    n = pl.program_id(1)
    blk = q_ref.shape[0]
    kb = jnp.concatenate([kp_ref[...], kc_ref[...]], axis=0).astype(F32)
    vb = jnp.concatenate([vp_ref[...], vc_ref[...]], axis=0).astype(F32)
    lane = _lane_iota((2 * blk, LANES))
    low = lane < HEAD_DIM
    qi = lax.broadcasted_iota(jnp.int32, (blk, 2 * blk), 0)
    kj = lax.broadcasted_iota(jnp.int32, (blk, 2 * blk), 1)
    rel = qi + blk - kj
    first_valid_key = jnp.where(n > 0, 0, blk)
    mask = (rel >= 0) & (rel < WINDOW) & (kj >= first_valid_key)
    qlow = _lane_iota((blk, LANES)) < HEAD_DIM
    contract_lanes = (((1,), (1,)), ((), ()))

    kv_cache = {}

    def kv_group(g):
        if g not in kv_cache:
            c, half = divmod(g, 2)
            kc = kb[:, c * LANES:(c + 1) * LANES]
            vc = vb[:, c * LANES:(c + 1) * LANES]
            kr = pltpu.roll(kc, HEAD_DIM, 1)
            vr = pltpu.roll(vc, HEAD_DIM, 1)
            k2 = (jnp.where(low, kc, kr) if half == 0 else jnp.where(low, kr, kc)).astype(BF16)
            v_lo, v_hi = (vc, vr) if half == 0 else (vr, vc)
            v_even = jnp.where(low, v_lo, 1.0).astype(BF16)
            v_odd = jnp.where(low, 1.0, v_hi).astype(BF16)
            kv_cache[g] = (k2, v_even, v_odd)
        return kv_cache[g]

    for p in range(heads // 2):
        k2 = kv_group((2 * p) // GQA_GROUP)[0]
        qp = q_ref[:, p * LANES:(p + 1) * LANES]
        for par in range(2):
            qh = jnp.where(qlow if par == 0 else ~qlow, qp, jnp.zeros_like(qp))
            s_scr[2 * p + par] = lax.dot_general(qh, k2, contract_lanes, preferred_element_type=F32)

    s = jnp.where(mask[None], s_scr[...], NEG)
    sink = sink_ref[...]
    m = jnp.broadcast_to(jnp.maximum(jnp.max(s, axis=-1, keepdims=True), sink), e_scr.shape)
    p_scr[...] = jnp.exp((s - jnp.concatenate([m] * (2 * blk // LANES), axis=-1)).astype(BF16))
    e_scr[...] = jnp.exp(sink - m)

    for p in range(heads // 2):
        _, v_even, v_odd = kv_group((2 * p) // GQA_GROUP)
        outs = []
        for par in range(2):
            hq = 2 * p + par
            acc = jnp.dot(p_scr[hq], v_even if par == 0 else v_odd, preferred_element_type=F32)
            outs.append(acc / (pltpu.roll(acc, HEAD_DIM, 1) + e_scr[hq]))
        o_ref[:, p * LANES:(p + 1) * LANES] = jnp.where(qlow, outs[0], outs[1]).astype(o_ref.dtype)


def _swa(qkv, sinks, batch, seq):
    t, n = qkv.shape
    heads = sinks.shape[-1]
    nq = heads * HEAD_DIM
    nkv = (n - nq) // 2
    blk = WINDOW
    nb = seq // blk
    kcol = nq // nkv
    return pl.pallas_call(
        functools.partial(_swa_kernel, heads=heads),
        grid=(batch, nb),
        in_specs=[
            pl.BlockSpec((blk, nq), lambda b, i: (b * nb + i, 0)),
            pl.BlockSpec((blk, nkv), lambda b, i: (b * nb + jnp.maximum(i - 1, 0), kcol)),
            pl.BlockSpec((blk, nkv), lambda b, i: (b * nb + i, kcol)),
            pl.BlockSpec((blk, nkv), lambda b, i: (b * nb + jnp.maximum(i - 1, 0), kcol + 1)),
            pl.BlockSpec((blk, nkv), lambda b, i: (b * nb + i, kcol + 1)),
            pl.BlockSpec((heads, 1, 1), lambda b, i: (0, 0, 0)),
        ],
        out_specs=pl.BlockSpec((blk, nq), lambda b, i: (b * nb + i, 0)),
        out_shape=jax.ShapeDtypeStruct((t, nq), BF16),
        scratch_shapes=[
            pltpu.VMEM((heads, blk, 2 * blk), F32),
            pltpu.VMEM((heads, blk, 2 * blk), BF16),
            pltpu.VMEM((heads, blk, LANES), F32),
        ],
        compiler_params=_params("parallel", "parallel"),
        name="swa_attention",
    )(qkv, qkv, qkv, qkv, qkv, sinks.reshape(heads, 1, 1))


AUG_HEADS = LANES // 4
AUG_Q0 = 3 * AUG_HEADS


def _split3(x):
    hi = x.astype(BF16).astype(F32)
    r1 = x - hi
    mid = r1.astype(BF16).astype(F32)
    lo = (r1 - mid).astype(BF16).astype(F32)
    return hi, mid, lo


def _fox_prep_kernel(gl_ref, f_ref, ka_ref, carry_ref):
    @pl.when(pl.program_id(1) == 0)
    def _():
        carry_ref[...] = jnp.zeros_like(carry_ref)

    ts = gl_ref.shape[0]
    logf = _log_sigmoid(gl_ref[...]) * LOG2E
    f = jnp.dot(_tril_f32(ts), logf, preferred_element_type=F32, precision=lax.Precision.HIGHEST) + carry_ref[...]
    f_ref[...] = f
    carry_ref[...] = f[ts - 1:ts, :]

    r = lax.broadcasted_iota(jnp.int32, (LANES, LANES), 0)
    c = lax.broadcasted_iota(jnp.int32, (LANES, LANES), 1)
    aug = jnp.where((_lane_iota(f.shape) >= AUG_Q0) & (_lane_iota(f.shape) < AUG_Q0 + 6), 1.0, 0.0)
    for p, part in enumerate(_split3(-f)):
        place = ((c == 3 * r + p) & (r < AUG_HEADS)).astype(BF16)
        aug = aug + jnp.dot(part.astype(BF16), place, preferred_element_type=F32)
    ka_ref[...] = aug.astype(BF16)


def _fox_prep(gate_logits, batch, seq, ts):
    ns = seq // ts
    return pl.pallas_call(
        _fox_prep_kernel,
        grid=(batch, ns),
        in_specs=[pl.BlockSpec((ts, LANES), lambda b, s: (b * ns + s, 0))],
        out_specs=[pl.BlockSpec((ts, LANES), lambda b, s: (b * ns + s, 0))] * 2,
        out_shape=[
            jax.ShapeDtypeStruct((batch * seq, LANES), F32),
            jax.ShapeDtypeStruct((batch * seq, LANES), BF16),
        ],
        scratch_shapes=[pltpu.VMEM((1, LANES), F32)],
        compiler_params=_params("parallel", "arbitrary"),
        name="fox_gate_cumsum",
    )(gate_logits)


def _fox_kernel(q_ref, k_ref, v_ref, f_ref, ka_ref, o_ref, m_scr, alpha_scr, acc_scr, s_scr, p_scr, ve_scr, vo_scr,
                *, tq, tk):
    hp = pl.program_id(1)
    i = pl.program_id(2)

    @pl.when(i == 0)
    def _():
        v = v_ref[...]
        vlow = _lane_iota(v.shape) < HEAD_DIM
        ones = jnp.ones_like(v)
        ve_scr[...] = jnp.where(vlow, v, ones)
        vo_scr[...] = jnp.where(vlow, ones, v)

    lane = _lane_iota((tq, LANES))
    low = lane < HEAD_DIM
    q = q_ref[...]
    f_tile = f_ref[...]
    contract_lanes = (((1,), (1,)), ((), ()))

    q_aug = []
    for par in range(2):
        head = 2 * hp + par
        fq = jnp.sum(jnp.where(lane == head, f_tile, 0.0), axis=-1, keepdims=True)
        hi, mid, lo = _split3(fq)
        a = jnp.where((lane >= 3 * head) & (lane < 3 * head + 3), 1.0, 0.0)
        a = jnp.where(lane == AUG_Q0 + 3 * par, hi, a)
        a = jnp.where(lane == AUG_Q0 + 3 * par + 1, mid, a)
        a = jnp.where(lane == AUG_Q0 + 3 * par + 2, lo, a)
        qh = jnp.where(low, q, jnp.zeros_like(q)) if par == 0 else jnp.where(low, jnp.zeros_like(q), q)
        q_aug.append(jnp.concatenate([qh, a.astype(BF16)], axis=1))

    m_scr[...] = jnp.full(m_scr.shape, NEG, F32)
    acc_scr[...] = jnp.zeros(acc_scr.shape, F32)
    n_full = (i * tq) // tk

    def tile(j, masked, r0=0):
        start = pl.multiple_of(j * tk, tk)
        k_aug = jnp.concatenate([k_ref[pl.ds(start, tk), :], ka_ref[pl.ds(start, tk), :]], axis=1)
        v_heads = (ve_scr[pl.ds(start, tk), :], vo_scr[pl.ds(start, tk), :])
        for par in range(2):
            s_scr[par, r0:, :] = lax.dot_general(q_aug[par][r0:, :], k_aug, contract_lanes,
                                                 preferred_element_type=F32)
        for par in range(2):
            z = s_scr[par, r0:, :]
            if masked:
                r = lax.broadcasted_iota(jnp.int32, z.shape, 0) + (i * tq + r0)
                c = lax.broadcasted_iota(jnp.int32, z.shape, 1) + start
                z = jnp.where(c <= r, z, NEG)
            m_prev = m_scr[par, r0:, :]
            m_new = jnp.maximum(m_prev, jnp.max(z, axis=-1, keepdims=True))
            alpha_scr[par, r0:, :] = jnp.exp2(m_prev - m_new)
            m_scr[par, r0:, :] = m_new
            p_scr[par, r0:, :] = jnp.exp2((z - jnp.concatenate([m_new] * (tk // LANES), axis=1)).astype(BF16))
        for par in range(2):
            acc_scr[par, r0:, :] = alpha_scr[par, r0:, :] * acc_scr[par, r0:, :] + jnp.dot(
                p_scr[par, r0:, :], v_heads[par], preferred_element_type=F32)

    def body(j, carry):
        tile(j, False)
        return carry

    lax.fori_loop(0, n_full, body, 0)
    for d in range(max(tq // tk, 1)):
        tile(n_full + d, True, r0=d * tk)

    a0 = acc_scr[0]
    a1 = acc_scr[1]
    o_ref[...] = jnp.where(low, a0 / pltpu.roll(a0, HEAD_DIM, 1), a1 / pltpu.roll(a1, HEAD_DIM, 1)).astype(o_ref.dtype)


def _fox(qkv, f, k_aug, batch, seq, heads, tk):
    assert heads <= AUG_HEADS
    t = qkv.shape[0]
    tq = _pick(seq, (1024, 512, 256, 128))
    nq = seq // tq
    pairs = heads // 2
    return pl.pallas_call(
        functools.partial(_fox_kernel, tq=tq, tk=tk),
        grid=(batch, pairs, nq),
        in_specs=[
            pl.BlockSpec((tq, LANES), lambda b, p, i: (b * nq + i, p)),
            pl.BlockSpec((seq, LANES), lambda b, p, i: (b, pairs + p)),
            pl.BlockSpec((seq, LANES), lambda b, p, i: (b, 2 * pairs + p)),
            pl.BlockSpec((tq, LANES), lambda b, p, i: (b * nq + i, 0)),
            pl.BlockSpec((seq, LANES), lambda b, p, i: (b, 0)),
        ],
        out_specs=pl.BlockSpec((tq, LANES), lambda b, p, i: (b * nq + i, p)),
        out_shape=jax.ShapeDtypeStruct((t, heads * HEAD_DIM), BF16),
        scratch_shapes=[
            pltpu.VMEM((2, tq, LANES), F32),
            pltpu.VMEM((2, tq, LANES), F32),
            pltpu.VMEM((2, tq, LANES), F32),
            pltpu.VMEM((2, tq, tk), F32),
            pltpu.VMEM((2, tq, tk), BF16),
            pltpu.VMEM((seq, LANES), BF16),
            pltpu.VMEM((seq, LANES), BF16),
        ],
        compiler_params=_params("parallel", "parallel", "arbitrary"),
        name="fox_attention",
    )(qkv, qkv, qkv, f, k_aug)


def _mlstm_kernel(q_ref, k_ref, v_ref, o_ref, g_ref, out_ref, c_scr, m_scr, *, dk, dv):
    @pl.when(pl.program_id(1) == 0)
    def _():
        c_scr[...] = jnp.zeros_like(c_scr)
        m_scr[...] = jnp.zeros_like(m_scr)

    L = g_ref.shape[0]
    gates = g_ref[...]
    b_all = jnp.dot(_tril_f32(L), _log_sigmoid(gates), preferred_element_type=F32,
                    precision=lax.Precision.HIGHEST)
    gates_t = gates.T
    b_all_t = b_all.T
    r = lax.broadcasted_iota(jnp.int32, (L, L), 0)
    c = lax.broadcasted_iota(jnp.int32, (L, L), 1)
    tri = c <= r
    ones_col = (_lane_iota((L, LANES)) == 0).astype(BF16)
    contract_lanes = (((1,), (1,)), ((), ()))
    contract_rows = (((0,), (0,)), ((), ()))

    for hd in range(C_HEADS):
        q = q_ref[:, hd * dk:(hd + 1) * dk]
        k = k_ref[:, hd * dk:(hd + 1) * dk]
        v_aug = jnp.concatenate([v_ref[:, hd * dv:(hd + 1) * dv], ones_col], axis=1)
        b_col = b_all[:, C_HEADS + hd:C_HEADS + hd + 1]
        i_col = gates[:, hd:hd + 1]
        b_row = b_all_t[C_HEADS + hd:C_HEADS + hd + 1, :]
        i_row = gates_t[hd:hd + 1, :]
        m_prev = m_scr[hd:hd + 1, 0:1]
        g_last = b_col[L - 1:L, :]

        dm = jnp.where(tri, b_col - b_row + i_row, NEG)
        inter = b_col + m_prev
        m_t = jnp.maximum(inter, jnp.max(dm, axis=-1, keepdims=True))
        a = jnp.exp(dm - m_t) * lax.dot_general(q, k, contract_lanes, preferred_element_type=F32)
        sc = jnp.exp(inter - m_t)
        state = c_scr[hd]
        num = sc * jnp.dot(q, state.astype(BF16), preferred_element_type=F32) \
            + jnp.dot(a.astype(BF16), v_aug, preferred_element_type=F32)
        den = num[:, dv:dv + 1]
        h_t = num[:, :dv] / jnp.maximum(jnp.abs(den), jnp.exp(-m_t))
        gate_o = jax.nn.sigmoid(o_ref[:, hd * dv:(hd + 1) * dv].astype(F32))
        out_ref[:, hd * dv:(hd + 1) * dv] = (gate_o * h_t).astype(out_ref.dtype)

        wlog = g_last - b_col + i_col
        m_new = jnp.maximum(g_last + m_prev, jnp.max(wlog, axis=0, keepdims=True))
        decay = jnp.exp(g_last + m_prev - m_new)
        kw = (k.astype(F32) * jnp.exp(wlog - m_new)).astype(BF16)
        c_scr[hd] = decay * state + lax.dot_general(kw, v_aug, contract_rows, preferred_element_type=F32)
        m_scr[hd:hd + 1, :] = jnp.broadcast_to(m_new, (1, LANES))


def _mlstm(proj, gates, batch, seq):
    t, n = proj.shape
    dv = n // (3 * C_HEADS)
    dk = dv // 2
    L = _pick(seq, (256, 128, 64))
    nc = seq // L
    hq = C_HEADS * dk
    hv = C_HEADS * dv
    qb, vb = hq // hq, (2 * hq) // hv
    return pl.pallas_call(
        functools.partial(_mlstm_kernel, dk=dk, dv=dv),
        grid=(batch, nc),
        in_specs=[
            pl.BlockSpec((L, hq), lambda b, s: (b * nc + s, 0)),
            pl.BlockSpec((L, hq), lambda b, s: (b * nc + s, qb)),
            pl.BlockSpec((L, hv), lambda b, s: (b * nc + s, vb)),
            pl.BlockSpec((L, hv), lambda b, s: (b * nc + s, vb + 1)),
            pl.BlockSpec((L, LANES), lambda b, s: (b * nc + s, 0)),
        ],
        out_specs=pl.BlockSpec((L, hv), lambda b, s: (b * nc + s, 0)),
        out_shape=jax.ShapeDtypeStruct((t, hv), BF16),
        scratch_shapes=[pltpu.VMEM((C_HEADS, dk, dv + LANES), F32), pltpu.VMEM((SUBLANES, LANES), F32)],
        compiler_params=_params("parallel", "arbitrary"),
        name="mlstm",
    )(proj, proj, proj, proj, gates)


def _split_gate_cols(w_in, bias, n_main):
    n_gate = w_in.shape[1] - n_main
    wg = jnp.zeros((w_in.shape[0], LANES), BF16).at[:, :n_gate].set(w_in[:, n_main:].astype(BF16))
    bg = jnp.zeros((1, LANES), F32).at[0, :n_gate].set(bias)
    return w_in[:, :n_main].astype(BF16), wg, bg


def kernel(x, c, positions, mod_w, mod_b, norm_g, final_g, a_w_in, a_sinks, a_w_out, b_w_in, b_f_bias, b_w_out, c_w_in, c_gate_bias, c_w_out, ffn_w_up, ffn_conv_w, ffn_conv_b, ffn_w_down):
    batch, seq, d = x.shape
    depth = mod_w.shape[0]
    t = batch * seq
    xf = x.reshape(t, d)

    mod = _modulation(c, mod_w, mod_b)[:, :batch, :].reshape(depth, batch, 6, 1, d)
    cos, sins = _rope_tables(positions)

    for i in range(depth):
        sh1, sc1, g1, sh2, sc2, g2 = (mod[i, :, r] for r in range(6))
        gain1 = norm_g[i, 0].reshape(1, d)
        gain2 = norm_g[i, 1].reshape(1, d)
        kind, j = i % N_MIXERS, i // N_MIXERS
        if kind == 0:
            qkv = _proj_a(xf, gain1, sh1, sc1, a_w_in[j].astype(BF16), cos, sins, seq)
            y = _swa(qkv, a_sinks[j], batch, seq)
            w_out = a_w_out[j]
        elif kind == 1:
            heads = b_f_bias.shape[-1]
            n_main = 3 * heads * HEAD_DIM
            w, wg, bg = _split_gate_cols(b_w_in[j], b_f_bias[j], n_main)
            qkv, gl = _proj_gate(xf, gain1, sh1, sc1, w, wg, bg, seq, heads * HEAD_DIM, HEAD_DIM ** -0.5 * LOG2E)
            tk = _pick(seq, (512, 256, 128))
            f, k_aug = _fox_prep(gl, batch, seq, tk)
            y = _fox(qkv, f, k_aug, batch, seq, heads, tk)
            w_out = b_w_out[j]
        else:
            n_main = c_w_in.shape[-1] - 2 * C_HEADS
            dk = n_main // (6 * C_HEADS)
            w, wg, bg = _split_gate_cols(c_w_in[j], c_gate_bias[j], n_main)
            proj, gl = _proj_gate(xf, gain1, sh1, sc1, w, wg, bg, seq, C_HEADS * dk, dk ** -0.5)
            y = _mlstm(proj, gl, batch, seq)
            w_out = c_w_out[j]
        xf = _proj_res(y, w_out.astype(BF16), xf, g1, seq)

        conv_tab = jnp.zeros((SUBLANES, ffn_conv_w.shape[-1]), F32)
        conv_tab = conv_tab.at[:CONV_WIDTH].set(ffn_conv_w[i]).at[CONV_WIDTH].set(ffn_conv_b[i])
        act = _ffn_up(xf, gain2, sh2, sc2, ffn_w_up[i].astype(BF16), conv_tab, seq)
        if i < depth - 1:
            xf = _proj_res(act, ffn_w_down[i].astype(BF16), xf, g2, seq)
        else:
            xf = _proj_res_norm(act, ffn_w_down[i].astype(BF16), xf, g2, final_g.reshape(1, d), seq)

    return xf.reshape(batch, seq, d)
```
